```python
import jax, jax.numpy as jnp
from jax import lax
import numpy as np

D_MODEL = 1024
BATCH = 32
SEQ = 256
DEPTH = 4
DEC_BATCH = 4
DEC_SEQ = 1024
PAST_LEN = 256

GRID_W = 64
N_HEADS = 16
HEAD_DIM = D_MODEL // N_HEADS
WIN_ROWS_MAX = 8
WIN_COLS = 16
Q_COL_BLOCK = WIN_COLS
K_COL_BLOCK = 2 * WIN_COLS
CONV_WIDTH = 31
D_FF = 4 * D_MODEL
N_MIXERS = 2
N_ATTN = (DEPTH + 1) // 2
N_CONV = DEPTH // 2
RMS_EPS = 1e-6
LN_EPS = 1e-5

kernel_name = "hybrid_natten_conformer_flow_step"


def rms_norm(x, g):
    xf = x.astype(jnp.float32)
    y = xf * lax.rsqrt(jnp.mean(xf * xf, axis=-1, keepdims=True) + RMS_EPS)
    return (y * g.astype(jnp.float32)).astype(x.dtype)


def layer_norm(x, g, b):
    xf = x.astype(jnp.float32)
    mu = jnp.mean(xf, axis=-1, keepdims=True)
    xc = xf - mu
    y = xc * lax.rsqrt(jnp.mean(xc * xc, axis=-1, keepdims=True) + LN_EPS)
    return (y * g.astype(jnp.float32) + b.astype(jnp.float32)).astype(x.dtype)


def adaln(cond, w, b):
    m = jax.nn.silu(cond) @ w + b
    return jnp.split(m, 6, axis=-1)


def modulate(h, shift, scale):
    return h * (1 + scale) + shift


def split_heads(x):
    return x.reshape(x.shape[0], x.shape[1], N_HEADS, HEAD_DIM)


def _na_indices(rows):
    kr = min(WIN_ROWS_MAX, rows)
    r = np.arange(rows)
    row_start = np.clip(r - kr // 2, 0, rows - kr)
    row_idx = row_start[:, None] + np.arange(kr)[None, :]
    n_cb = GRID_W // Q_COL_BLOCK
    cb = np.arange(n_cb)
    kcol_start = np.clip(cb * Q_COL_BLOCK - WIN_COLS // 2, 0, GRID_W - K_COL_BLOCK)
    col_idx = kcol_start[:, None] + np.arange(K_COL_BLOCK)[None, :]
    qcol = cb[:, None] * Q_COL_BLOCK + np.arange(Q_COL_BLOCK)[None, :]
    win_start = np.clip(qcol - WIN_COLS // 2, 0, GRID_W - WIN_COLS)
    kc = col_idx[:, None, :]
    col_valid = (kc >= win_start[..., None]) & (kc < win_start[..., None] + WIN_COLS)
    mask = np.broadcast_to(col_valid[:, :, None, :], (n_cb, Q_COL_BLOCK, kr, K_COL_BLOCK))
    mask = mask.reshape(n_cb, Q_COL_BLOCK, kr * K_COL_BLOCK)
    d_row = row_idx - r[:, None] + (WIN_ROWS_MAX - 1)
    d_col = np.clip(kc - qcol[..., None] + (WIN_COLS - 1), 0, 2 * WIN_COLS - 2)
    return kr, n_cb, row_idx, col_idx, mask, d_row, d_col


def na_context(h, w_qkv, w_o):
    q, k, v = jnp.split(h @ w_qkv, 3, axis=-1)
    q, k, v = split_heads(q), split_heads(k), split_heads(v)
    s = jnp.einsum('bqhd,bkhd->bhqk', q, k).astype(jnp.float32) * (HEAD_DIM ** -0.5)
    p = jax.nn.softmax(s, axis=-1).astype(v.dtype)
    o = jnp.einsum('bhqk,bkhd->bqhd', p, v).reshape(h.shape[0], h.shape[1], D_MODEL)
    return o @ w_o, k, v


def na_latent(h, k_ctx, v_ctx, w_qkv, w_o, rpb):
    b, t = h.shape[0], h.shape[1]
    rows = t // GRID_W
    kr, n_cb, row_idx, col_idx, mask, d_row, d_col = _na_indices(rows)
    q, k, v = jnp.split(h @ w_qkv, 3, axis=-1)
    qg = q.reshape(b, rows, n_cb, Q_COL_BLOCK, N_HEADS, HEAD_DIM)
    kg = k.reshape(b, rows, GRID_W, N_HEADS, HEAD_DIM)
    vg = v.reshape(b, rows, GRID_W, N_HEADS, HEAD_DIM)
    ri = row_idx[:, None, :, None]
    ci = col_idx[None, :, None, :]
    n_loc = kr * K_COL_BLOCK
    kb = kg[:, ri, ci].reshape(b, rows, n_cb, n_loc, N_HEADS, HEAD_DIM)
    vb = vg[:, ri, ci].reshape(b, rows, n_cb, n_loc, N_HEADS, HEAD_DIM)
    rel = rpb[:, d_row[:, None, None, :, None], d_col[None, :, :, None, :]]
    rel = rel.reshape(N_HEADS, rows, n_cb, Q_COL_BLOCK, n_loc).astype(jnp.float32)
    bias = jnp.where(mask, rel, -jnp.inf)
    scale = HEAD_DIM ** -0.5
    s_loc = jnp.einsum('brjqhd,brjkhd->bhrjqk', qg, kb).astype(jnp.float32) * scale + bias[None]
    s_ctx = jnp.einsum('brjqhd,blhd->bhrjql', qg, k_ctx).astype(jnp.float32) * scale
    p = jax.nn.softmax(jnp.concatenate([s_loc, s_ctx], axis=-1), axis=-1).astype(v.dtype)
    o = (jnp.einsum('bhrjqk,brjkhd->brjqhd', p[..., :n_loc], vb)
         + jnp.einsum('bhrjql,blhd->brjqhd', p[..., n_loc:], v_ctx))
    return o.reshape(b, t, D_MODEL) @ w_o


def conv_module(h, w_pw1, w_dw, b_dw, ln_g, ln_b, w_pw2):
    a, g = jnp.split(h @ w_pw1, 2, axis=-1)
    u = a * jax.nn.sigmoid(g)
    u = lax.conv_general_dilated(
        u, w_dw[:, None, :].astype(u.dtype), window_strides=(1,),
        padding=[(CONV_WIDTH // 2, CONV_WIDTH // 2)],
        dimension_numbers=('NWC', 'WIO', 'NWC'), feature_group_count=D_MODEL) + b_dw
    u = jax.nn.silu(layer_norm(u, ln_g, ln_b))
    return u @ w_pw2


def sq_relu_mlp(h, w_up, w_down):
    return jnp.square(jax.nn.relu(h @ w_up)) @ w_down


def setup_inputs(seed: int = 0) -> dict:
    key = jax.random.key(seed)
    ks = jax.random.split(key, 21)
    d = D_MODEL

    def nrm(k, shape, s):
        return jax.random.normal(k, shape, jnp.float32) * s

    return {
        "x_prompt": nrm(ks[0], (BATCH, SEQ, d), 1.0),
        "x_sample": nrm(ks[1], (DEC_BATCH, DEC_SEQ, d), 1.0),
        "cache_k": nrm(ks[2], (DEC_BATCH, N_ATTN, PAST_LEN, N_HEADS, HEAD_DIM), 1.0),
        "cache_v": nrm(ks[3], (DEC_BATCH, N_ATTN, PAST_LEN, N_HEADS, HEAD_DIM), 1.0),
        "c": nrm(ks[4], (DEC_BATCH, d), 1.0),
        "c_ctx": nrm(ks[5], (d,), 1.0),
        "norm_g": 1.0 + nrm(ks[6], (DEPTH, 2, d), 0.02),
        "w_ada": nrm(ks[7], (DEPTH, d, 6 * d), 0.5 * d ** -0.5),
        "b_ada": nrm(ks[8], (DEPTH, 6 * d), 0.01),
        "w_qkv": nrm(ks[9], (N_ATTN, d, 3 * d), d ** -0.5),
        "w_o": nrm(ks[10], (N_ATTN, d, d), d ** -0.5),
        "rpb": nrm(ks[11], (N_ATTN, N_HEADS, 2 * WIN_ROWS_MAX - 1, 2 * WIN_COLS - 1), 0.1),
        "w_pw1": nrm(ks[12], (N_CONV, d, 2 * d), d ** -0.5),
        "w_dw": nrm(ks[13], (N_CONV, CONV_WIDTH, d), CONV_WIDTH ** -0.5),
        "b_dw": nrm(ks[14], (N_CONV, d), 0.01),
        "conv_ln_g": 1.0 + nrm(ks[15], (N_CONV, d), 0.02),
        "conv_ln_b": nrm(ks[16], (N_CONV, d), 0.01),
        "w_pw2": nrm(ks[17], (N_CONV, d, d), d ** -0.5),
        "w_up": nrm(ks[18], (DEPTH, d, D_FF), d ** -0.5),
        "w_down": nrm(ks[19], (DEPTH, D_FF, d), D_FF ** -0.5),
        "final_g": 1.0 + nrm(ks[20], (d,), 0.02),
    }


def reference(x_prompt, x_sample, cache_k, cache_v, c, c_ctx, norm_g, w_ada, b_ada,
              w_qkv, w_o, rpb, w_pw1, w_dw, b_dw, conv_ln_g, conv_ln_b, w_pw2,
              w_up, w_down, final_g):
    xp, xs = x_prompt, x_sample
    new_k, new_v = [], []
    for l in range(DEPTH):
        i = l // N_MIXERS
        sh1c, sc1c, g1c, sh2c, sc2c, g2c = adaln(c_ctx, w_ada[l], b_ada[l])
        sh1, sc1, g1, sh2, sc2, g2 = [m[:, None, :] for m in adaln(c, w_ada[l], b_ada[l])]
        hp = modulate(rms_norm(xp, norm_g[l, 0]), sh1c, sc1c)
        hs = modulate(rms_norm(xs, norm_g[l, 0]), sh1, sc1)
        if l % N_MIXERS == 0:
            yp, kp, vp = na_context(hp, w_qkv[i], w_o[i])
            new_k.append(kp)
            new_v.append(vp)
            ys = na_latent(hs, cache_k[:, i], cache_v[:, i], w_qkv[i], w_o[i], rpb[i])
        else:
            yp = conv_module(hp, w_pw1[i], w_dw[i], b_dw[i], conv_ln_g[i], conv_ln_b[i], w_pw2[i])
            ys = conv_module(hs, w_pw1[i], w_dw[i], b_dw[i], conv_ln_g[i], conv_ln_b[i], w_pw2[i])
        xp = xp + g1c * yp
        xs = xs + g1 * ys
        hp = modulate(rms_norm(xp, norm_g[l, 1]), sh2c, sc2c)
        hs = modulate(rms_norm(xs, norm_g[l, 1]), sh2, sc2)
        xp = xp + g2c * sq_relu_mlp(hp, w_up[l], w_down[l])
        xs = xs + g2 * sq_relu_mlp(hs, w_up[l], w_down[l])
    y_prompt = rms_norm(xp, final_g)
    y_sample = rms_norm(xs, final_g)
    new_cache_k = jnp.stack(new_k, axis=1)
    new_cache_v = jnp.stack(new_v, axis=1)
    return (y_prompt, y_sample, new_cache_k, new_cache_v)
```

```python
import functools

import numpy as np
import jax
import jax.numpy as jnp
from jax import lax
from jax.experimental import pallas as pl
from jax.experimental.pallas import tpu as pltpu

F32 = jnp.float32
BF16 = jnp.bfloat16

RMS_EPS = 1e-6
LN_EPS = 1e-5
N_HEADS = 16
GRID_W = 64
WIN_ROWS = 8
WIN_COLS = 16
CONV_WIDTH = 31
CONV_PAD = 16
N_COND_ROWS = 8

LANE = 128
ATT_BLOCK = 256
VMEM_LIMIT = 56 * 1024 * 1024


def _rms_mod(xf, g, shift, scale):
    y = xf * lax.rsqrt(jnp.mean(xf * xf, axis=-1, keepdims=True) + RMS_EPS)
    return (y * g) * (1.0 + scale) + shift


def _cond_row(i, n_prompt_tiles, tiles_per_latent_seq):
    return jnp.where(i < n_prompt_tiles, 0, 1 + (i - n_prompt_tiles) // tiles_per_latent_seq)


def _const_spec(shape):
    nd = len(shape)
    return pl.BlockSpec(shape, lambda *_: (0,) * nd, pipeline_mode=pl.Buffered(1))


def _adaln_kernel(cond_ref, w_ref, b_ref, o_ref):
    s = cond_ref[...]
    s = s * jax.nn.sigmoid(s)
    o_ref[0] = jnp.dot(s.astype(BF16), w_ref[0].astype(BF16), preferred_element_type=F32) + b_ref[0]


def _adaln(cond, w_ada, b_ada):
    depth, d, n6 = w_ada.shape
    tn = n6 // 4
    return pl.pallas_call(
        _adaln_kernel,
        grid=(depth, n6 // tn),
        in_specs=[
            pl.BlockSpec((N_COND_ROWS, d), lambda l, j: (0, 0)),
            pl.BlockSpec((1, d, tn), lambda l, j: (l, 0, j)),
            pl.BlockSpec((1, 1, tn), lambda l, j: (l, 0, j)),
        ],
        out_specs=pl.BlockSpec((1, N_COND_ROWS, tn), lambda l, j: (l, 0, j)),
        out_shape=jax.ShapeDtypeStruct((depth, N_COND_ROWS, n6), F32),
        compiler_params=pltpu.CompilerParams(
            dimension_semantics=("arbitrary", "arbitrary"), vmem_limit_bytes=VMEM_LIMIT),
        name="adaln",
    )(cond, w_ada, b_ada.reshape(depth, 1, n6))


def _qkv_kernel(x_ref, mod_ref, g_ref, w_ref, q_ref, k_ref, v_ref, kf_ref, vf_ref, *, n_prompt_tiles, scale):
    d = x_ref.shape[1]
    h = _rms_mod(x_ref[...], g_ref[...], mod_ref[0:1, :], mod_ref[1:2, :])
    qkv = jnp.dot(h.astype(BF16), w_ref[...], preferred_element_type=F32)
    k = qkv[:, d:2 * d]
    v = qkv[:, 2 * d:]
    q_ref[...] = (qkv[:, :d] * scale).astype(BF16)
    k_ref[...] = k.astype(BF16)
    v_ref[...] = v.astype(BF16)

    @pl.when(pl.program_id(0) < n_prompt_tiles)
    def _():
        kf_ref[...] = k
        vf_ref[...] = v


def _qkv(x, mod_l, g, w_qkv, *, n_prompt, latent_seq, tm, scale):
    n_tok, d = x.shape
    npt = n_prompt // tm
    row = functools.partial(_cond_row, n_prompt_tiles=npt, tiles_per_latent_seq=latent_seq // tm)
    tok = pl.BlockSpec((tm, d), lambda i: (i, 0))
    prompt_only = pl.BlockSpec((tm, d), lambda i: (jnp.minimum(i, npt - 1), 0))
    return pl.pallas_call(
        functools.partial(_qkv_kernel, n_prompt_tiles=npt, scale=scale),
        grid=(n_tok // tm,),
        in_specs=[
            tok,
            pl.BlockSpec((None, 6, d), lambda i: (row(i), 0, 0)),
            _const_spec((1, d)),
            _const_spec((d, 3 * d)),
        ],
        out_specs=[tok, tok, tok, prompt_only, prompt_only],
        out_shape=[jax.ShapeDtypeStruct((n_tok, d), BF16)] * 3
        + [jax.ShapeDtypeStruct((n_prompt, d), F32)] * 2,
        compiler_params=pltpu.CompilerParams(
            dimension_semantics=("arbitrary",), vmem_limit_bytes=VMEM_LIMIT),
        name="qkv",
    )(x, mod_l, g, w_qkv)


def _head_masks(shape):
    lane = lax.broadcasted_iota(jnp.int32, shape, 1)
    return lane < (LANE // 2)


def _nt_dot(a, b):
    return lax.dot_general(a, b, (((1,), (1,)), ((), ())), preferred_element_type=F32)


def _prompt_attn_kernel(q_ref, k_ref, v_ref, o_ref):
    seq, d = q_ref.shape
    first = _head_masks((seq, LANE))
    zero = jnp.zeros((seq, LANE), BF16)
    for p in range(d // LANE):
        sl = slice(p * LANE, (p + 1) * LANE)
        qp, kp, vp = q_ref[:, sl], k_ref[:, sl], v_ref[:, sl]
        outs = []
        for hh in range(2):
            qm = jnp.where(first, qp, zero) if hh == 0 else jnp.where(first, zero, qp)
            s = _nt_dot(qm, kp)
            m = jnp.max(s, axis=-1, keepdims=True)
            e = jnp.exp(s - m)
            l = jnp.sum(e, axis=-1, keepdims=True)
            outs.append(jnp.dot(e.astype(BF16), vp, preferred_element_type=F32) / l)
        o_ref[:, sl] = jnp.where(first, outs[0], outs[1]).astype(BF16)


def _prompt_attn(q, k, v, *, n_prompt, seq):
    d = q.shape[1]
    blk = pl.BlockSpec((seq, d), lambda b: (b, 0))
    return pl.pallas_call(
        _prompt_attn_kernel,
        grid=(n_prompt // seq,),
        in_specs=[blk, blk, blk],
        out_specs=blk,
        out_shape=jax.ShapeDtypeStruct((n_prompt, d), BF16),
        compiler_params=pltpu.CompilerParams(
            dimension_semantics=("arbitrary",), vmem_limit_bytes=VMEM_LIMIT),
        name="prompt_attn",
    )(q, k, v)


def _latent_block_plan(rows):
    rows_per_block = ATT_BLOCK // GRID_W
    n_blocks = rows // rows_per_block
    kr = min(WIN_ROWS, rows)
    plan = []
    for qb in range(n_blocks):
        r = np.arange(qb * rows_per_block, (qb + 1) * rows_per_block)
        start = np.clip(r - kr // 2, 0, rows - kr)
        lo = int(start.min()) // rows_per_block
        hi = int((start.max() + kr - 1)) // rows_per_block
        plan.append((lo, hi + 1))
    return plan


def _latent_bias_index(rows):
    plan = _latent_block_plan(rows)
    kr = min(WIN_ROWS, rows)
    pos = np.arange(ATT_BLOCK)
    d_rows, d_cols, valids = [], [], []
    for qb, (lo, hi) in enumerate(plan):
        q_row = (qb * ATT_BLOCK + pos) // GRID_W
        q_col = pos % GRID_W
        row_start = np.clip(q_row - kr // 2, 0, rows - kr)
        col_start = np.clip(q_col - WIN_COLS // 2, 0, GRID_W - WIN_COLS)
        for kb in range(lo, hi):
            k_row = (kb * ATT_BLOCK + pos) // GRID_W
            k_col = pos % GRID_W
            dr = k_row[None, :] - q_row[:, None] + (WIN_ROWS - 1)
            dc = k_col[None, :] - q_col[:, None] + (WIN_COLS - 1)
            ok = ((k_row[None, :] >= row_start[:, None]) & (k_row[None, :] < row_start[:, None] + kr)
                  & (k_col[None, :] >= col_start[:, None]) & (k_col[None, :] < col_start[:, None] + WIN_COLS))
            d_rows.append(np.clip(dr, 0, 2 * WIN_ROWS - 2))
            d_cols.append(np.clip(dc, 0, 2 * WIN_COLS - 2))
            valids.append(ok)
    return plan, np.stack(d_rows), np.stack(d_cols), np.stack(valids)


def _latent_attn_kernel(q_ref, k_ref, v_ref, kc_ref, vc_ref, bias_ref, o_ref, *, plan):
    first = _head_masks((ATT_BLOCK, LANE))
    zero = jnp.zeros((ATT_BLOCK, LANE), BF16)
    kc = kc_ref[...].astype(BF16)
    vc = vc_ref[...].astype(BF16)
    bias_at = 0
    for qb, (lo, hi) in enumerate(plan):
        qp = q_ref[qb * ATT_BLOCK:(qb + 1) * ATT_BLOCK, :]
        k_loc = k_ref[lo * ATT_BLOCK:hi * ATT_BLOCK, :]
        v_loc = v_ref[lo * ATT_BLOCK:hi * ATT_BLOCK, :]
        outs = []
        for hh in range(2):
            qm = jnp.where(first, qp, zero) if hh == 0 else jnp.where(first, zero, qp)
            bias = jnp.concatenate([bias_ref[hh, bias_at + j] for j in range(hi - lo)], axis=1)
            s_loc = _nt_dot(qm, k_loc) + bias
            s_ctx = _nt_dot(qm, kc)
            m = jnp.maximum(jnp.max(s_loc, axis=-1, keepdims=True), jnp.max(s_ctx, axis=-1, keepdims=True))
            e_loc = jnp.exp(s_loc - m)
            e_ctx = jnp.exp(s_ctx - m)
            l = jnp.sum(e_loc, axis=-1, keepdims=True) + jnp.sum(e_ctx, axis=-1, keepdims=True)
            o = (jnp.dot(e_loc.astype(BF16), v_loc, preferred_element_type=F32)
                 + jnp.dot(e_ctx.astype(BF16), vc, preferred_element_type=F32))
            outs.append(o / l)
        bias_at += hi - lo
        o_ref[qb * ATT_BLOCK:(qb + 1) * ATT_BLOCK, :] = jnp.where(first, outs[0], outs[1]).astype(BF16)


def _latent_attn(q, k, v, cache_k, cache_v, bias, *, layer_i, n_prompt, latent_seq, plan):
    d = q.shape[1]
    n_batch, _, past, _ = cache_k.shape
    n_pairs = d // LANE
    first_latent = n_prompt // latent_seq
    tok = pl.BlockSpec((latent_seq, LANE), lambda p, b: (first_latent + b, p))
    ctx = pl.BlockSpec((None, None, past, LANE), lambda p, b: (b, layer_i, 0, p))
    n_bias = bias.shape[1]
    return pl.pallas_call(
        functools.partial(_latent_attn_kernel, plan=plan),
        grid=(n_pairs, n_batch),
        in_specs=[
            tok, tok, tok, ctx, ctx,
            pl.BlockSpec((2, n_bias, ATT_BLOCK, ATT_BLOCK), lambda p, b: (p, 0, 0, 0)),
        ],
        out_specs=pl.BlockSpec((latent_seq, LANE), lambda p, b: (b, p)),
        out_shape=jax.ShapeDtypeStruct((n_batch * latent_seq, d), BF16),
        compiler_params=pltpu.CompilerParams(
            dimension_semantics=("arbitrary", "arbitrary"), vmem_limit_bytes=VMEM_LIMIT),
        name="latent_attn",
    )(q, k, v, cache_k, cache_v, bias)


def _conv_kernel(x_ref, mod_ref, g_ref, w1_ref, wdw_ref, bdw_ref, lng_ref, lnb_ref, a_ref, pad_ref,
                 *, n_prompt_tiles, prompt_seq, rows_per_step, lane_chunk):
    tm, d = x_ref.shape
    h = _rms_mod(x_ref[...], g_ref[...], mod_ref[0:1, :], mod_ref[1:2, :])
    ag = jnp.dot(h.astype(BF16), w1_ref[...], preferred_element_type=F32)
    u = ag[:, :d] * jax.nn.sigmoid(ag[:, d:])

    def conv(seq):
        stride = seq + 2 * CONV_PAD
        for s in range(tm // seq):
            base = s * stride
            pad_ref[base:base + CONV_PAD, :] = jnp.zeros((CONV_PAD, d), F32)
            pad_ref[base + CONV_PAD:base + CONV_PAD + seq, :] = u[s * seq:(s + 1) * seq, :]
            pad_ref[base + CONV_PAD + seq:base + stride, :] = jnp.zeros((CONV_PAD, d), F32)
        first_tap = CONV_PAD - CONV_WIDTH // 2
        for s in range(tm // seq):
            base = s * stride

            def step(ci, carry, base=base, s=s):
                r0 = pl.multiple_of(ci * rows_per_step, rows_per_step)
                parts = []
                for lc in range(d // lane_chunk):
                    ls = slice(lc * lane_chunk, (lc + 1) * lane_chunk)
                    win = pad_ref[pl.ds(base + r0, rows_per_step + 2 * CONV_PAD), ls]
                    part = jnp.broadcast_to(bdw_ref[:, ls], (rows_per_step, lane_chunk))
                    for t in range(CONV_WIDTH):
                        part = part + win[first_tap + t:first_tap + t + rows_per_step, :] * wdw_ref[t:t + 1, ls]
                    parts.append(part)
                acc = jnp.concatenate(parts, axis=1)
                mu = jnp.mean(acc, axis=-1, keepdims=True)
                xc = acc - mu
                y = xc * lax.rsqrt(jnp.mean(xc * xc, axis=-1, keepdims=True) + LN_EPS)
                y = y * lng_ref[...] + lnb_ref[...]
                a_ref[pl.ds(s * seq + r0, rows_per_step), :] = (y * jax.nn.sigmoid(y)).astype(BF16)
                return carry

            lax.fori_loop(0, seq // rows_per_step, step, 0)

    is_prompt = pl.program_id(0) < n_prompt_tiles
    pl.when(is_prompt)(lambda: conv(prompt_seq))
    pl.when(jnp.logical_not(is_prompt))(lambda: conv(tm))


def _conv_front(x, mod_l, g, w_pw1, w_dw, b_dw, ln_g, ln_b, *, n_prompt, prompt_seq, latent_seq):
    n_tok, d = x.shape
    tm = latent_seq
    npt = n_prompt // tm
    row = functools.partial(_cond_row, n_prompt_tiles=npt, tiles_per_latent_seq=1)
    tok = pl.BlockSpec((tm, d), lambda i: (i, 0))
    pad_rows = (tm // prompt_seq) * (prompt_seq + 2 * CONV_PAD)
    return pl.pallas_call(
        functools.partial(_conv_kernel, n_prompt_tiles=npt, prompt_seq=prompt_seq, rows_per_step=32,
                          lane_chunk=2 * LANE),
        grid=(n_tok // tm,),
        in_specs=[
            tok,
            pl.BlockSpec((None, 6, d), lambda i: (row(i), 0, 0)),
            _const_spec((1, d)),
            _const_spec((d, 2 * d)),
            _const_spec((CONV_WIDTH, d)),
            _const_spec((1, d)),
            _const_spec((1, d)),
            _const_spec((1, d)),
        ],
        out_specs=tok,
        out_shape=jax.ShapeDtypeStruct((n_tok, d), BF16),
        scratch_shapes=[pltpu.VMEM((pad_rows, d), F32)],
        compiler_params=pltpu.CompilerParams(
            dimension_semantics=("arbitrary",), vmem_limit_bytes=VMEM_LIMIT),
        name="conv_front",
    )(x, mod_l, g, w_pw1, w_dw, b_dw, ln_g, ln_b)


def _post_kernel(x_ref, a_ref, mod_ref, g_ref, wp_ref, wup_ref, wdn_ref, fg_ref, o_ref, *, ff_chunk, final):
    d = x_ref.shape[1]
    y = jnp.dot(a_ref[...], wp_ref[...], preferred_element_type=F32)
    x1 = x_ref[...] + mod_ref[2:3, :] * y
    h = _rms_mod(x1, g_ref[...], mod_ref[3:4, :], mod_ref[4:5, :]).astype(BF16)
    acc = jnp.zeros_like(x1)
    for c in range(wup_ref.shape[1] // ff_chunk):
        sl = slice(c * ff_chunk, (c + 1) * ff_chunk)
        u = jnp.maximum(jnp.dot(h, wup_ref[:, sl], preferred_element_type=F32), 0.0)
        acc = acc + jnp.dot((u * u).astype(BF16), wdn_ref[sl, :], preferred_element_type=F32)
    out = x1 + mod_ref[5:6, :] * acc
    if final:
        out = out * lax.rsqrt(jnp.mean(out * out, axis=-1, keepdims=True) + RMS_EPS) * fg_ref[...]
    o_ref[...] = out


def _post(x, a, mod_l, g, w_proj, w_up, w_down, final_g, *, n_prompt, latent_seq, tm, final):
    n_tok, d = x.shape
    d_ff = w_up.shape[1]
    npt = n_prompt // tm
    row = functools.partial(_cond_row, n_prompt_tiles=npt, tiles_per_latent_seq=latent_seq // tm)
    tok = pl.BlockSpec((tm, d), lambda i: (i, 0))
    return pl.pallas_call(
        functools.partial(_post_kernel, ff_chunk=1024, final=final),
        grid=(n_tok // tm,),
        in_specs=[
            tok, tok,
            pl.BlockSpec((None, 6, d), lambda i: (row(i), 0, 0)),
            _const_spec((1, d)),
            _const_spec((d, d)),
            _const_spec((d, d_ff)),
            _const_spec((d_ff, d)),
            _const_spec((1, d)),
        ],
        out_specs=tok,
        out_shape=jax.ShapeDtypeStruct((n_tok, d), F32),
        compiler_params=pltpu.CompilerParams(
            dimension_semantics=("arbitrary",), vmem_limit_bytes=VMEM_LIMIT),
        name="post_mlp",
    )(x, a, mod_l, g, w_proj, w_up, w_down, final_g)


def kernel(x_prompt, x_sample, cache_k, cache_v, c, c_ctx, norm_g, w_ada, b_ada, w_qkv, w_o, rpb,
           w_pw1, w_dw, b_dw, conv_ln_g, conv_ln_b, w_pw2, w_up, w_down, final_g):
    batch, seq, d = x_prompt.shape
    dec_batch, dec_seq, _ = x_sample.shape
    depth = w_ada.shape[0]
    n_attn = w_qkv.shape[0]
    past = cache_k.shape[2]
    n_prompt = batch * seq
    head_dim = d // N_HEADS
    rows = dec_seq // GRID_W
    assert d == N_HEADS * head_dim and 2 * head_dim == LANE
    assert seq == ATT_BLOCK and past == ATT_BLOCK and dec_seq % ATT_BLOCK == 0
    assert 1 + dec_batch <= N_COND_ROWS
    tm = 512
    dims = dict(n_prompt=n_prompt, latent_seq=dec_seq)

    x = jnp.concatenate([x_prompt.reshape(n_prompt, d), x_sample.reshape(dec_batch * dec_seq, d)], axis=0)
    cond = jnp.concatenate([c_ctx[None, :], c, jnp.zeros((N_COND_ROWS - 1 - dec_batch, d), F32)], axis=0)
    mod = _adaln(cond, w_ada, b_ada).reshape(depth, N_COND_ROWS, 6, d)

    plan, d_row, d_col, valid = _latent_bias_index(rows)
    ck = cache_k.reshape(dec_batch, n_attn, past, d)
    cv = cache_v.reshape(dec_batch, n_attn, past, d)
    final_g2 = final_g.reshape(1, d)

    new_k, new_v = [], []
    for l in range(depth):
        i = l // 2
        mod_l = mod[l]
        g1 = norm_g[l, 0].reshape(1, d)
        g2 = norm_g[l, 1].reshape(1, d)
        if l % 2 == 0:
            q, k, v, kf, vf = _qkv(x, mod_l, g1, w_qkv[i].astype(BF16), tm=tm, scale=head_dim ** -0.5, **dims)
            new_k.append(kf.reshape(batch, seq, N_HEADS, head_dim))
            new_v.append(vf.reshape(batch, seq, N_HEADS, head_dim))
            bias = jnp.where(valid[None], rpb[i][:, d_row, d_col], -jnp.inf)
            o_p = _prompt_attn(q, k, v, n_prompt=n_prompt, seq=seq)
            o_s = _latent_attn(q, k, v, ck, cv, bias, layer_i=i, plan=plan, **dims)
            a = jnp.concatenate([o_p, o_s], axis=0)
            w_proj = w_o[i]
        else:
            a = _conv_front(x, mod_l, g1, w_pw1[i].astype(BF16), w_dw[i], b_dw[i].reshape(1, d),
                            conv_ln_g[i].reshape(1, d), conv_ln_b[i].reshape(1, d), prompt_seq=seq, **dims)
            w_proj = w_pw2[i]
        x = _post(x, a, mod_l, g2, w_proj.astype(BF16), w_up[l].astype(BF16), w_down[l].astype(BF16),
                  final_g2, tm=tm, final=(l == depth - 1), **dims)

    y_prompt = x[:n_prompt].reshape(batch, seq, d)
    y_sample = x[n_prompt:].reshape(dec_batch, dec_seq, d)
    return (y_prompt, y_sample, jnp.stack(new_k, axis=1), jnp.stack(new_v, axis=1))
```

```python
import functools

import numpy as np
import jax
import jax.numpy as jnp
from jax import lax
from jax.experimental import pallas as pl
from jax.experimental.pallas import tpu as pltpu

F32 = jnp.float32
BF16 = jnp.bfloat16

RMS_EPS = 1e-6
LN_EPS = 1e-5
N_HEADS = 16
GRID_W = 64
WIN_ROWS = 8
WIN_COLS = 16
N_DROW = 2 * WIN_ROWS - 1
N_DCOL = 2 * WIN_COLS - 1
CONV_WIDTH = 31
CONV_PAD = 16
N_COND_ROWS = 8

LANE = 128
SUBLANE = 8
ATT_BLOCK = 256
ROWS_PER_BLOCK = ATT_BLOCK // GRID_W
VMEM_LIMIT = 56 * 1024 * 1024


def _rms_mod(xf, g, shift, scale):
    y = xf * lax.rsqrt(jnp.mean(xf * xf, axis=-1, keepdims=True) + RMS_EPS)
    return (y * g) * (1.0 + scale) + shift


def _cond_row(i, n_prompt_tiles, tiles_per_latent_seq):
    return jnp.where(i < n_prompt_tiles, 0, 1 + (i - n_prompt_tiles) // tiles_per_latent_seq)


def _const_spec(shape):
    nd = len(shape)
    return pl.BlockSpec(shape, lambda *_: (0,) * nd, pipeline_mode=pl.Buffered(1))


def _token_specs(parts, tm, n_prompt_tiles):
    d = parts[0].shape[1]
    if len(parts) == 1:
        return [pl.BlockSpec((tm, d), lambda i: (i, 0))]
    return [pl.BlockSpec((tm, d), lambda i: (jnp.minimum(i, n_prompt_tiles - 1), 0)),
            pl.BlockSpec((tm, d), lambda i: (jnp.maximum(i - n_prompt_tiles, 0), 0))]


def _token_tile(refs, is_prompt):
    if len(refs) == 1:
        return refs[0][...]
    return jnp.where(is_prompt, refs[0][...], refs[1][...])


def _adaln_kernel(cond_ref, w_ref, b_ref, o_ref):
    s = cond_ref[...]
    s = s * jax.nn.sigmoid(s)
    o_ref[0] = jnp.dot(s.astype(BF16), w_ref[0].astype(BF16), preferred_element_type=F32) + b_ref[0]


def _adaln(cond, w_ada, b_ada):
    depth, d, n6 = w_ada.shape
    tn = n6 // 4
    return pl.pallas_call(
        _adaln_kernel,
        grid=(depth, n6 // tn),
        in_specs=[
            pl.BlockSpec((N_COND_ROWS, d), lambda l, j: (0, 0)),
            pl.BlockSpec((1, d, tn), lambda l, j: (l, 0, j)),
            pl.BlockSpec((1, 1, tn), lambda l, j: (l, 0, j)),
        ],
        out_specs=pl.BlockSpec((1, N_COND_ROWS, tn), lambda l, j: (l, 0, j)),
        out_shape=jax.ShapeDtypeStruct((depth, N_COND_ROWS, n6), F32),
        compiler_params=pltpu.CompilerParams(
            dimension_semantics=("arbitrary", "arbitrary"), vmem_limit_bytes=VMEM_LIMIT),
        name="adaln",
    )(cond, w_ada, b_ada.reshape(depth, 1, n6))


def _qkv_kernel(*refs, n_x, n_prompt_tiles, scale):
    x_refs = refs[:n_x]
    mod_ref, g_ref, w_ref, q_ref, k_ref, v_ref, kf_ref, vf_ref = refs[n_x:]
    d = q_ref.shape[1]
    is_prompt = pl.program_id(0) < n_prompt_tiles
    h = _rms_mod(_token_tile(x_refs, is_prompt), g_ref[...], mod_ref[0:1, :], mod_ref[1:2, :])
    qkv = jnp.dot(h.astype(BF16), w_ref[...], preferred_element_type=F32)
    k = qkv[:, d:2 * d]
    v = qkv[:, 2 * d:]
    q_ref[...] = (qkv[:, :d] * scale).astype(BF16)
    k_ref[...] = k.astype(BF16)
    v_ref[...] = v.astype(BF16)

    @pl.when(is_prompt)
    def _():
        kf_ref[...] = k
        vf_ref[...] = v


def _qkv(x_parts, mod_l, g, w_qkv, *, n_prompt, n_tok, latent_seq, tm, scale):
    d = x_parts[0].shape[1]
    npt = n_prompt // tm
    row = functools.partial(_cond_row, n_prompt_tiles=npt, tiles_per_latent_seq=latent_seq // tm)
    tok = pl.BlockSpec((tm, d), lambda i: (i, 0))
    prompt_only = pl.BlockSpec((tm, d), lambda i: (jnp.minimum(i, npt - 1), 0))
    return pl.pallas_call(
        functools.partial(_qkv_kernel, n_x=len(x_parts), n_prompt_tiles=npt, scale=scale),
        grid=(n_tok // tm,),
        in_specs=_token_specs(x_parts, tm, npt) + [
            pl.BlockSpec((None, 6, d), lambda i: (row(i), 0, 0)),
            _const_spec((1, d)),
            _const_spec((d, 3 * d)),
        ],
        out_specs=[tok, tok, tok, prompt_only, prompt_only],
        out_shape=[jax.ShapeDtypeStruct((n_tok, d), BF16)] * 3
        + [jax.ShapeDtypeStruct((n_prompt, d), F32)] * 2,
        compiler_params=pltpu.CompilerParams(
            dimension_semantics=("arbitrary",), vmem_limit_bytes=VMEM_LIMIT),
        name="qkv",
    )(*x_parts, mod_l, g, w_qkv)


def _nt_dot(a, b):
    return lax.dot_general(a, b, (((1,), (1,)), ((), ())), preferred_element_type=F32)


def _pair_heads(qp, first):
    zero = jnp.zeros_like(qp)
    return jnp.where(first, qp, zero), jnp.where(first, zero, qp)


def _prompt_attn_kernel(q_ref, k_ref, v_ref, o_ref):
    seq, d = q_ref.shape
    first = lax.broadcasted_iota(jnp.int32, (seq, LANE), 1) < (LANE // 2)
    for p in range(d // LANE):
        sl = slice(p * LANE, (p + 1) * LANE)
        kp, vp = k_ref[:, sl], v_ref[:, sl]
        outs = []
        for qm in _pair_heads(q_ref[:, sl], first):
            s = _nt_dot(qm, kp)
            e = jnp.exp(s - jnp.max(s, axis=-1, keepdims=True))
            l = jnp.sum(e, axis=-1, keepdims=True)
            outs.append(jnp.dot(e.astype(BF16), vp, preferred_element_type=F32) / l)
        o_ref[:, sl] = jnp.where(first, outs[0], outs[1]).astype(BF16)


def _prompt_attn(q, k, v, *, n_prompt, seq):
    d = q.shape[1]
    blk = pl.BlockSpec((seq, d), lambda b: (b, 0))
    return pl.pallas_call(
        _prompt_attn_kernel,
        grid=(n_prompt // seq,),
        in_specs=[blk, blk, blk],
        out_specs=blk,
        out_shape=jax.ShapeDtypeStruct((n_prompt, d), BF16),
        compiler_params=pltpu.CompilerParams(
            dimension_semantics=("arbitrary",), vmem_limit_bytes=VMEM_LIMIT),
        name="prompt_attn",
    )(q, k, v)


def _row_start(r, rows):
    kr = min(WIN_ROWS, rows)
    return min(max(r - kr // 2, 0), rows - kr)


def _latent_block_plan(rows):
    kr = min(WIN_ROWS, rows)
    plan = []
    for qb in range(rows // ROWS_PER_BLOCK):
        starts = [_row_start(r, rows) for r in range(qb * ROWS_PER_BLOCK, (qb + 1) * ROWS_PER_BLOCK)]
        plan.append((min(starts) // ROWS_PER_BLOCK, (max(starts) + kr - 1) // ROWS_PER_BLOCK + 1))
    return plan


def _build_pair_bias(rpb_ref, tile_ref, bias_ref, head0, *, plan, rows):
    kr = min(WIN_ROWS, rows)
    shape = (GRID_W, LANE)
    qc = lax.broadcasted_iota(jnp.int32, shape, 0)
    lane = lax.broadcasted_iota(jnp.int32, shape, 1)
    kc = lane & (GRID_W - 1)
    delta = kc - qc + (WIN_COLS - 1)
    col_start = jnp.clip(qc - WIN_COLS // 2, 0, GRID_W - WIN_COLS)
    in_window = (kc >= col_start) & (kc < col_start + WIN_COLS)
    left = lane < GRID_W
    neg = jnp.full(shape, -jnp.inf, F32)
    for hh in range(2):
        base = (head0 + hh) * (N_DROW * N_DCOL)
        for dr in range(N_DROW):
            w = neg
            for dc in range(N_DCOL):
                w = jnp.where(delta == dc, rpb_ref[base + dr * N_DCOL + dc], w)
            tile_ref[dr] = jnp.where(in_window, w, neg)
        t = 0
        for qb, (lo, hi) in enumerate(plan):
            for kb in range(lo, hi):
                for a in range(ROWS_PER_BLOCK):
                    r = qb * ROWS_PER_BLOCK + a
                    start = _row_start(r, rows)
                    for c in range(ATT_BLOCK // LANE):
                        halves = []
                        for jj in range(LANE // GRID_W):
                            k_row = kb * ROWS_PER_BLOCK + c * (LANE // GRID_W) + jj
                            ok = start <= k_row < start + kr
                            halves.append(tile_ref[k_row - r + WIN_ROWS - 1] if ok else None)
                        if halves[0] is None and halves[1] is None:
                            val = neg
                        else:
                            val = jnp.where(left, neg if halves[0] is None else halves[0],
                                            neg if halves[1] is None else halves[1])
                        bias_ref[hh, t, a * GRID_W:(a + 1) * GRID_W, c * LANE:(c + 1) * LANE] = val
                t += 1


def _latent_attn_kernel(rpb_ref, q_ref, k_ref, v_ref, kc_ref, vc_ref, o_ref, tile_ref, bias_ref, *, plan, rows):
    @pl.when(pl.program_id(1) == 0)
    def _():
        _build_pair_bias(rpb_ref, tile_ref, bias_ref, 2 * pl.program_id(0), plan=plan, rows=rows)

    first = lax.broadcasted_iota(jnp.int32, (ATT_BLOCK, LANE), 1) < (LANE // 2)
    kc = kc_ref[...].astype(BF16)
    vc = vc_ref[...].astype(BF16)
    bias_at = 0
    for qb, (lo, hi) in enumerate(plan):
        k_loc = k_ref[lo * ATT_BLOCK:hi * ATT_BLOCK, :]
        v_loc = v_ref[lo * ATT_BLOCK:hi * ATT_BLOCK, :]
        outs = []
        for hh, qm in enumerate(_pair_heads(q_ref[qb * ATT_BLOCK:(qb + 1) * ATT_BLOCK, :], first)):
            bias = jnp.concatenate([bias_ref[hh, bias_at + j] for j in range(hi - lo)], axis=1)
            s_loc = _nt_dot(qm, k_loc) + bias
            s_ctx = _nt_dot(qm, kc)
            m = jnp.maximum(jnp.max(s_loc, axis=-1, keepdims=True), jnp.max(s_ctx, axis=-1, keepdims=True))
            e_loc = jnp.exp(s_loc - m)
            e_ctx = jnp.exp(s_ctx - m)
            l = jnp.sum(e_loc, axis=-1, keepdims=True) + jnp.sum(e_ctx, axis=-1, keepdims=True)
            o = (jnp.dot(e_loc.astype(BF16), v_loc, preferred_element_type=F32)
                 + jnp.dot(e_ctx.astype(BF16), vc, preferred_element_type=F32))
            outs.append(o / l)
        bias_at += hi - lo
        o_ref[qb * ATT_BLOCK:(qb + 1) * ATT_BLOCK, :] = jnp.where(first, outs[0], outs[1]).astype(BF16)


def _latent_attn(rpb_flat, q, k, v, cache_k, cache_v, *, layer_i, n_prompt, latent_seq):
    d = q.shape[1]
    n_batch, _, past, _ = cache_k.shape
    rows = latent_seq // GRID_W
    plan = _latent_block_plan(rows)
    n_bias = sum(hi - lo for lo, hi in plan)
    first_latent = n_prompt // latent_seq
    tok = pl.BlockSpec((latent_seq, LANE), lambda p, b: (first_latent + b, p))
    ctx = pl.BlockSpec((None, None, past, LANE), lambda p, b: (b, layer_i, 0, p))
    return pl.pallas_call(
        functools.partial(_latent_attn_kernel, plan=plan, rows=rows),
        grid=(d // LANE, n_batch),
        in_specs=[pl.BlockSpec(memory_space=pltpu.SMEM), tok, tok, tok, ctx, ctx],
        out_specs=pl.BlockSpec((latent_seq, LANE), lambda p, b: (b, p)),
        out_shape=jax.ShapeDtypeStruct((n_batch * latent_seq, d), BF16),
        scratch_shapes=[pltpu.VMEM((N_DROW, GRID_W, LANE), F32),
                        pltpu.VMEM((2, n_bias, ATT_BLOCK, ATT_BLOCK), F32)],
        compiler_params=pltpu.CompilerParams(
            dimension_semantics=("arbitrary", "arbitrary"), vmem_limit_bytes=VMEM_LIMIT),
        name="latent_attn",
    )(rpb_flat, q, k, v, cache_k, cache_v)


def _conv_kernel(x_ref, mod_ref, g_ref, w1_ref, wdw_ref, bdw_ref, lng_ref, lnb_ref, a_ref, pad_ref,
                 *, n_prompt_tiles, prompt_seq, rows_per_step):
    tm, d = x_ref.shape
    h = _rms_mod(x_ref[...], g_ref[...], mod_ref[0:1, :], mod_ref[1:2, :])
    ag = jnp.dot(h.astype(BF16), w1_ref[...], preferred_element_type=F32)
    u = ag[:, :d] * jax.nn.sigmoid(ag[:, d:])
    first_tap = CONV_PAD - CONV_WIDTH // 2
    rs = rows_per_step

    def conv(seq):
        stride = seq + 2 * CONV_PAD
        for s in range(tm // seq):
            base = s * stride
            pad_ref[base:base + CONV_PAD, :] = jnp.zeros((CONV_PAD, d), F32)
            pad_ref[base + CONV_PAD:base + CONV_PAD + seq, :] = u[s * seq:(s + 1) * seq, :]
            pad_ref[base + CONV_PAD + seq:base + stride, :] = jnp.zeros((CONV_PAD, d), F32)
        for s in range(tm // seq):
            base = s * stride

            def step(ci, carry, base=base, s=s):
                r0 = pl.multiple_of(ci * rs, rs)
                parts = []
                for lc in range(d // LANE):
                    ls = slice(lc * LANE, (lc + 1) * LANE)
                    win = pad_ref[pl.ds(base + r0, rs + 2 * CONV_PAD), ls]
                    acc = jnp.broadcast_to(bdw_ref[:, ls], (rs, LANE))
                    for b in range(SUBLANE):
                        vb = None
                        for o in range(b, CONV_WIDTH + first_tap, SUBLANE):
                            t = o - first_tap
                            if t < 0:
                                continue
                            term = win[o - b:o - b + rs + SUBLANE, :] * wdw_ref[t:t + 1, ls]
                            vb = term if vb is None else vb + term
                        if b:
                            vb = pltpu.roll(vb, rs + SUBLANE - b, axis=0)
                        acc = acc + vb[:rs, :]
                    parts.append(acc)
                acc = jnp.concatenate(parts, axis=1)
                mu = jnp.mean(acc, axis=-1, keepdims=True)
                xc = acc - mu
                y = xc * lax.rsqrt(jnp.mean(xc * xc, axis=-1, keepdims=True) + LN_EPS)
                y = y * lng_ref[...] + lnb_ref[...]
                a_ref[pl.ds(s * seq + r0, rs), :] = (y * jax.nn.sigmoid(y)).astype(BF16)
                return carry

            lax.fori_loop(0, seq // rs, step, 0)

    is_prompt = pl.program_id(0) < n_prompt_tiles
    pl.when(is_prompt)(lambda: conv(prompt_seq))
    pl.when(jnp.logical_not(is_prompt))(lambda: conv(tm))


def _conv_front(x, mod_l, g, w_pw1, w_dw, b_dw, ln_g, ln_b, *, n_prompt, prompt_seq, latent_seq):
    n_tok, d = x.shape
    tm = latent_seq
    npt = n_prompt // tm
    row = functools.partial(_cond_row, n_prompt_tiles=npt, tiles_per_latent_seq=1)
    tok = pl.BlockSpec((tm, d), lambda i: (i, 0))
    pad_rows = (tm // prompt_seq) * (prompt_seq + 2 * CONV_PAD)
    return pl.pallas_call(
        functools.partial(_conv_kernel, n_prompt_tiles=npt, prompt_seq=prompt_seq, rows_per_step=64),
        grid=(n_tok // tm,),
        in_specs=[
            tok,
            pl.BlockSpec((None, 6, d), lambda i: (row(i), 0, 0)),
            _const_spec((1, d)),
            _const_spec((d, 2 * d)),
            _const_spec((CONV_WIDTH, d)),
            _const_spec((1, d)),
            _const_spec((1, d)),
            _const_spec((1, d)),
        ],
        out_specs=tok,
        out_shape=jax.ShapeDtypeStruct((n_tok, d), BF16),
        scratch_shapes=[pltpu.VMEM((pad_rows, d), F32)],
        compiler_params=pltpu.CompilerParams(
            dimension_semantics=("arbitrary",), vmem_limit_bytes=VMEM_LIMIT),
        name="conv_front",
    )(x, mod_l, g, w_pw1, w_dw, b_dw, ln_g, ln_b)


def _post_kernel(*refs, n_x, n_a, n_out, n_prompt_tiles, ff_chunk, final):
    x_refs, a_refs = refs[:n_x], refs[n_x:n_x + n_a]
    mod_ref, g_ref, wp_ref, wup_ref, wdn_ref, fg_ref = refs[n_x + n_a:n_x + n_a + 6]
    o_refs = refs[n_x + n_a + 6:]
    is_prompt = pl.program_id(0) < n_prompt_tiles
    y = jnp.dot(_token_tile(a_refs, is_prompt), wp_ref[...], preferred_element_type=F32)
    x1 = _token_tile(x_refs, is_prompt) + mod_ref[2:3, :] * y
    h = _rms_mod(x1, g_ref[...], mod_ref[3:4, :], mod_ref[4:5, :]).astype(BF16)
    acc = jnp.zeros_like(x1)
    for c in range(wup_ref.shape[1] // ff_chunk):
        sl = slice(c * ff_chunk, (c + 1) * ff_chunk)
        u = jnp.maximum(jnp.dot(h, wup_ref[:, sl], preferred_element_type=F32), 0.0)
        acc = acc + jnp.dot((u * u).astype(BF16), wdn_ref[sl, :], preferred_element_type=F32)
    out = x1 + mod_ref[5:6, :] * acc
    if final:
        out = out * lax.rsqrt(jnp.mean(out * out, axis=-1, keepdims=True) + RMS_EPS) * fg_ref[...]
    if n_out == 1:
        o_refs[0][...] = out
    else:
        @pl.when(is_prompt)
        def _():
            o_refs[0][...] = out

        @pl.when(jnp.logical_not(is_prompt))
        def _():
            o_refs[1][...] = out


def _post(x_parts, a_parts, mod_l, g, w_proj, w_up, w_down, final_g, *, n_prompt, n_tok, latent_seq, tm, final):
    d = x_parts[0].shape[1]
    d_ff = w_up.shape[1]
    npt = n_prompt // tm
    row = functools.partial(_cond_row, n_prompt_tiles=npt, tiles_per_latent_seq=latent_seq // tm)
    if final:
        out_shape = [jax.ShapeDtypeStruct((n_prompt, d), F32), jax.ShapeDtypeStruct((n_tok - n_prompt, d), F32)]
    else:
        out_shape = [jax.ShapeDtypeStruct((n_tok, d), F32)]
    return pl.pallas_call(
        functools.partial(_post_kernel, n_x=len(x_parts), n_a=len(a_parts), n_out=len(out_shape),
                          n_prompt_tiles=npt, ff_chunk=1024, final=final),
        grid=(n_tok // tm,),
        in_specs=_token_specs(x_parts, tm, npt) + _token_specs(a_parts, tm, npt) + [
            pl.BlockSpec((None, 6, d), lambda i: (row(i), 0, 0)),
            _const_spec((1, d)),
            _const_spec((d, d)),
            _const_spec((d, d_ff)),
            _const_spec((d_ff, d)),
            _const_spec((1, d)),
        ],
        out_specs=_token_specs(out_shape, tm, npt),
        out_shape=out_shape,
        compiler_params=pltpu.CompilerParams(
            dimension_semantics=("arbitrary",), vmem_limit_bytes=VMEM_LIMIT),
        name="post_mlp",
    )(*x_parts, *a_parts, mod_l, g, w_proj, w_up, w_down, final_g)


def kernel(x_prompt, x_sample, cache_k, cache_v, c, c_ctx, norm_g, w_ada, b_ada, w_qkv, w_o, rpb,
           w_pw1, w_dw, b_dw, conv_ln_g, conv_ln_b, w_pw2, w_up, w_down, final_g):
    batch, seq, d = x_prompt.shape
    dec_batch, dec_seq, _ = x_sample.shape
    depth = w_ada.shape[0]
    n_attn = w_qkv.shape[0]
    past = cache_k.shape[2]
    n_prompt = batch * seq
    n_tok = n_prompt + dec_batch * dec_seq
    head_dim = d // N_HEADS
    assert d == N_HEADS * head_dim and 2 * head_dim == LANE and LANE == 2 * GRID_W
    assert seq == ATT_BLOCK and past == ATT_BLOCK and dec_seq % ATT_BLOCK == 0
    assert rpb.shape[1:] == (N_HEADS, N_DROW, N_DCOL)
    assert 1 + dec_batch <= N_COND_ROWS
    tm = 512
    dims = dict(n_prompt=n_prompt, n_tok=n_tok, latent_seq=dec_seq)

    x_parts = [x_prompt.reshape(n_prompt, d), x_sample.reshape(dec_batch * dec_seq, d)]
    cond = jnp.concatenate([c_ctx[None, :], c, jnp.zeros((N_COND_ROWS - 1 - dec_batch, d), F32)], axis=0)
    mod = _adaln(cond, w_ada, b_ada).reshape(depth, N_COND_ROWS, 6, d)
    ck = cache_k.reshape(dec_batch, n_attn, past, d)
    cv = cache_v.reshape(dec_batch, n_attn, past, d)
    final_g2 = final_g.reshape(1, d)

    new_k, new_v = [], []
    for l in range(depth):
        i = l // 2
        mod_l = mod[l]
        g1 = norm_g[l, 0].reshape(1, d)
        g2 = norm_g[l, 1].reshape(1, d)
        if l % 2 == 0:
            q, k, v, kf, vf = _qkv(x_parts, mod_l, g1, w_qkv[i].astype(BF16), tm=tm, scale=head_dim ** -0.5, **dims)
            new_k.append(kf.reshape(batch, seq, N_HEADS, head_dim))
            new_v.append(vf.reshape(batch, seq, N_HEADS, head_dim))
            o_p = _prompt_attn(q, k, v, n_prompt=n_prompt, seq=seq)
            o_s = _latent_attn(rpb[i].reshape(-1), q, k, v, ck, cv, layer_i=i, n_prompt=n_prompt, latent_seq=dec_seq)
            a_parts = [o_p, o_s]
            w_proj = w_o[i]
        else:
            assert len(x_parts) == 1
            a_parts = [_conv_front(x_parts[0], mod_l, g1, w_pw1[i].astype(BF16), w_dw[i], b_dw[i].reshape(1, d),
                                   conv_ln_g[i].reshape(1, d), conv_ln_b[i].reshape(1, d), n_prompt=n_prompt,
                                   prompt_seq=seq, latent_seq=dec_seq)]
            w_proj = w_pw2[i]
        x_parts = _post(x_parts, a_parts, mod_l, g2, w_proj.astype(BF16), w_up[l].astype(BF16),
                        w_down[l].astype(BF16), final_g2, tm=tm, final=(l == depth - 1), **dims)

    y_prompt, y_sample = x_parts
    return (y_prompt.reshape(batch, seq, d), y_sample.reshape(dec_batch, dec_seq, d),
            jnp.stack(new_k, axis=1), jnp.stack(new_v, axis=1))
```

```python
import functools

import numpy as np
import jax
import jax.numpy as jnp
from jax import lax
from jax.experimental import pallas as pl
from jax.experimental.pallas import tpu as pltpu

F32 = jnp.float32
BF16 = jnp.bfloat16

RMS_EPS = 1e-6
LN_EPS = 1e-5
N_HEADS = 16
GRID_W = 64
WIN_ROWS = 8
WIN_COLS = 16
N_DROW = 2 * WIN_ROWS - 1
N_DCOL = 2 * WIN_COLS - 1
CONV_WIDTH = 31
CONV_PAD = 16
N_COND_ROWS = 8

LANE = 128
SUBLANE = 8
ATT_BLOCK = 256
ROWS_PER_BLOCK = ATT_BLOCK // GRID_W
VMEM_LIMIT = 56 * 1024 * 1024


def _rms_mod(xf, g, shift, scale):
    y = xf * lax.rsqrt(jnp.mean(xf * xf, axis=-1, keepdims=True) + RMS_EPS)
    return (y * g) * (1.0 + scale) + shift


def _cond_row(i, n_prompt_tiles, tiles_per_latent_seq):
    return jnp.where(i < n_prompt_tiles, 0, 1 + (i - n_prompt_tiles) // tiles_per_latent_seq)


def _const_spec(shape):
    nd = len(shape)
    return pl.BlockSpec(shape, lambda *_: (0,) * nd, pipeline_mode=pl.Buffered(1))


def _token_specs(parts, tm, n_prompt_tiles):
    d = parts[0].shape[1]
    if len(parts) == 1:
        return [pl.BlockSpec((tm, d), lambda i: (i, 0))]
    return [pl.BlockSpec((tm, d), lambda i: (jnp.minimum(i, n_prompt_tiles - 1), 0)),
            pl.BlockSpec((tm, d), lambda i: (jnp.maximum(i - n_prompt_tiles, 0), 0))]


def _token_tile(refs, is_prompt):
    if len(refs) == 1:
        return refs[0][...]
    return jnp.where(is_prompt, refs[0][...], refs[1][...])


def _adaln_kernel(cond_ref, w_ref, b_ref, o_ref):
    s = cond_ref[...]
    s = s * jax.nn.sigmoid(s)
    o_ref[0] = jnp.dot(s.astype(BF16), w_ref[0].astype(BF16), preferred_element_type=F32) + b_ref[0]


def _adaln(cond, w_ada, b_ada):
    depth, d, n6 = w_ada.shape
    tn = n6 // 4
    return pl.pallas_call(
        _adaln_kernel,
        grid=(depth, n6 // tn),
        in_specs=[
            pl.BlockSpec((N_COND_ROWS, d), lambda l, j: (0, 0)),
            pl.BlockSpec((1, d, tn), lambda l, j: (l, 0, j)),
            pl.BlockSpec((1, 1, tn), lambda l, j: (l, 0, j)),
        ],
        out_specs=pl.BlockSpec((1, N_COND_ROWS, tn), lambda l, j: (l, 0, j)),
        out_shape=jax.ShapeDtypeStruct((depth, N_COND_ROWS, n6), F32),
        compiler_params=pltpu.CompilerParams(
            dimension_semantics=("arbitrary", "arbitrary"), vmem_limit_bytes=VMEM_LIMIT),
        name="adaln",
    )(cond, w_ada, b_ada.reshape(depth, 1, n6))


def _qkv_kernel(*refs, n_x, n_prev, layer_i, n_prompt_tiles, scale):
    x_refs = refs[:n_x]
    mod_ref, g_ref, w_ref, wt_ref = refs[n_x:n_x + 4]
    q_ref, k_ref, v_ref, kt_ref, vt_ref = refs[n_x + 4 + n_prev:]
    d = q_ref.shape[1]
    n_seq, seq = kt_ref.shape[0], kt_ref.shape[-1]
    is_prompt = pl.program_id(0) < n_prompt_tiles
    h = _rms_mod(_token_tile(x_refs, is_prompt), g_ref[...], mod_ref[0:1, :], mod_ref[1:2, :]).astype(BF16)

    @pl.when(is_prompt)
    def _():
        q_ref[...] = (jnp.dot(h, w_ref[:, :d], preferred_element_type=F32) * scale).astype(BF16)
        for s in range(n_seq):
            hs = h[s * seq:(s + 1) * seq, :]
            for t_ref, lo in ((kt_ref, 0), (vt_ref, d)):
                t = _nt_dot(wt_ref[lo:lo + d, :], hs)
                if n_prev:
                    t_ref[s] = t
                else:
                    for j in range(t_ref.shape[1]):
                        t_ref[s, j] = t if j == layer_i else jnp.zeros_like(t)

    @pl.when(jnp.logical_not(is_prompt))
    def _():
        qkv = jnp.dot(h, w_ref[...], preferred_element_type=F32)
        q_ref[...] = (qkv[:, :d] * scale).astype(BF16)
        k_ref[...] = qkv[:, d:2 * d].astype(BF16)
        v_ref[...] = qkv[:, 2 * d:].astype(BF16)


def _qkv(x_parts, mod_l, g, w_qkv, w_kv_t, prev_caches, *, layer_i, n_attn, seq, n_prompt, n_tok, latent_seq,
         tm, scale):
    d = x_parts[0].shape[1]
    npt = n_prompt // tm
    row = functools.partial(_cond_row, n_prompt_tiles=npt, tiles_per_latent_seq=latent_seq // tm)
    tok = pl.BlockSpec((tm, d), lambda i: (i, 0))
    latent_only = pl.BlockSpec((tm, d), lambda i: (jnp.maximum(i - npt, 0), 0))
    if prev_caches:
        cache_blk = pl.BlockSpec((tm // seq, None, d, seq), lambda i: (jnp.minimum(i, npt - 1), layer_i, 0, 0))
    else:
        cache_blk = pl.BlockSpec((tm // seq, n_attn, d, seq), lambda i: (jnp.minimum(i, npt - 1), 0, 0, 0))
    cache_shape = jax.ShapeDtypeStruct((n_prompt // seq, n_attn, d, seq), F32)
    n_in = len(x_parts) + 4
    return pl.pallas_call(
        functools.partial(_qkv_kernel, n_x=len(x_parts), n_prev=len(prev_caches), layer_i=layer_i,
                          n_prompt_tiles=npt, scale=scale),
        grid=(n_tok // tm,),
        in_specs=_token_specs(x_parts, tm, npt) + [
            pl.BlockSpec((None, 6, d), lambda i: (row(i), 0, 0)),
            _const_spec((1, d)),
            _const_spec((d, 3 * d)),
            _const_spec((2 * d, d)),
        ] + [pl.BlockSpec(memory_space=pl.ANY)] * len(prev_caches),
        out_specs=[tok, latent_only, latent_only, cache_blk, cache_blk],
        out_shape=[jax.ShapeDtypeStruct((n_tok, d), BF16)]
        + [jax.ShapeDtypeStruct((n_tok - n_prompt, d), BF16)] * 2 + [cache_shape] * 2,
        input_output_aliases={n_in + j: 3 + j for j in range(len(prev_caches))},
        compiler_params=pltpu.CompilerParams(
            dimension_semantics=("arbitrary",), vmem_limit_bytes=VMEM_LIMIT),
        name="qkv",
    )(*x_parts, mod_l, g, w_qkv, w_kv_t, *prev_caches)


def _nt_dot(a, b):
    return lax.dot_general(a, b, (((1,), (1,)), ((), ())), preferred_element_type=F32)


def _pair_heads(qp, first):
    zero = jnp.zeros_like(qp)
    return jnp.where(first, qp, zero), jnp.where(first, zero, qp)


def _prompt_attn_kernel(q_ref, kt_ref, vt_ref, o_ref):
    seq, d = q_ref.shape
    half = LANE // 2
    first = lax.broadcasted_iota(jnp.int32, (seq, LANE), 1) < half
    ones = jnp.ones((2 * SUBLANE, seq), BF16)
    for p in range(d // LANE):
        sl = slice(p * LANE, (p + 1) * LANE)
        ktp = kt_ref[sl, :].astype(BF16)
        vtp = jnp.concatenate([vt_ref[sl, :].astype(BF16), ones], axis=0)
        outs = []
        for qm in _pair_heads(q_ref[:, sl], first):
            s = jnp.dot(qm, ktp, preferred_element_type=F32)
            e = jnp.exp(s - jnp.max(s, axis=-1, keepdims=True))
            ot = _nt_dot(vtp, e.astype(BF16))
            outs.append(ot[:LANE, :] / ot[LANE:LANE + 1, :])
        ot = jnp.concatenate([outs[0][:half, :], outs[1][half:, :]], axis=0)
        o_ref[:, sl] = ot.T.astype(BF16)


def _prompt_attn(q, cache_kt, cache_vt, *, layer_i, n_prompt, seq):
    d = q.shape[1]
    tok = pl.BlockSpec((seq, d), lambda b: (b, 0))
    feat = pl.BlockSpec((None, None, d, seq), lambda b: (b, layer_i, 0, 0))
    return pl.pallas_call(
        _prompt_attn_kernel,
        grid=(n_prompt // seq,),
        in_specs=[tok, feat, feat],
        out_specs=tok,
        out_shape=jax.ShapeDtypeStruct((n_prompt, d), BF16),
        compiler_params=pltpu.CompilerParams(
            dimension_semantics=("arbitrary",), vmem_limit_bytes=VMEM_LIMIT),
        name="prompt_attn",
    )(q, cache_kt, cache_vt)


def _row_start(r, rows):
    kr = min(WIN_ROWS, rows)
    return min(max(r - kr // 2, 0), rows - kr)


def _latent_block_plan(rows):
    kr = min(WIN_ROWS, rows)
    plan = []
    for qb in range(rows // ROWS_PER_BLOCK):
        starts = [_row_start(r, rows) for r in range(qb * ROWS_PER_BLOCK, (qb + 1) * ROWS_PER_BLOCK)]
        plan.append((min(starts) // ROWS_PER_BLOCK, (max(starts) + kr - 1) // ROWS_PER_BLOCK + 1))
    return plan


def _build_pair_bias(rpb_ref, tile_ref, bias_ref, head0, *, plan, rows):
    kr = min(WIN_ROWS, rows)
    shape = (GRID_W, LANE)
    qc = lax.broadcasted_iota(jnp.int32, shape, 0)
    lane = lax.broadcasted_iota(jnp.int32, shape, 1)
    kc = lane & (GRID_W - 1)
    delta = kc - qc + (WIN_COLS - 1)
    col_start = jnp.clip(qc - WIN_COLS // 2, 0, GRID_W - WIN_COLS)
    in_window = (kc >= col_start) & (kc < col_start + WIN_COLS)
    left = lane < GRID_W
    neg = jnp.full(shape, -jnp.inf, F32)
    for hh in range(2):
        base = (head0 + hh) * (N_DROW * N_DCOL)
        for dr in range(N_DROW):
            w = neg
            for dc in range(N_DCOL):
                w = jnp.where(delta == dc, rpb_ref[base + dr * N_DCOL + dc], w)
            tile_ref[dr] = jnp.where(in_window, w, neg)
        t = 0
        for qb, (lo, hi) in enumerate(plan):
            for kb in range(lo, hi):
                for a in range(ROWS_PER_BLOCK):
                    r = qb * ROWS_PER_BLOCK + a
                    start = _row_start(r, rows)
                    for c in range(ATT_BLOCK // LANE):
                        halves = []
                        for jj in range(LANE // GRID_W):
                            k_row = kb * ROWS_PER_BLOCK + c * (LANE // GRID_W) + jj
                            ok = start <= k_row < start + kr
                            halves.append(tile_ref[k_row - r + WIN_ROWS - 1] if ok else None)
                        if halves[0] is None and halves[1] is None:
                            val = neg
                        else:
                            val = jnp.where(left, neg if halves[0] is None else halves[0],
                                            neg if halves[1] is None else halves[1])
                        bias_ref[hh, t, a * GRID_W:(a + 1) * GRID_W, c * LANE:(c + 1) * LANE] = val
                t += 1


def _latent_attn_kernel(rpb_ref, q_ref, k_ref, v_ref, kct_ref, vct_ref, o_ref, tile_ref, bias_ref, *, plan, rows):
    @pl.when(pl.program_id(1) == 0)
    def _():
        _build_pair_bias(rpb_ref, tile_ref, bias_ref, 2 * pl.program_id(0), plan=plan, rows=rows)

    first = lax.broadcasted_iota(jnp.int32, (ATT_BLOCK, LANE), 1) < (LANE // 2)
    kct = kct_ref[...].astype(BF16)
    vct = vct_ref[...].astype(BF16)
    bias_at = 0
    for qb, (lo, hi) in enumerate(plan):
        k_loc = k_ref[lo * ATT_BLOCK:hi * ATT_BLOCK, :]
        v_loc = v_ref[lo * ATT_BLOCK:hi * ATT_BLOCK, :]
        outs = []
        for hh, qm in enumerate(_pair_heads(q_ref[qb * ATT_BLOCK:(qb + 1) * ATT_BLOCK, :], first)):
            bias = jnp.concatenate([bias_ref[hh, bias_at + j] for j in range(hi - lo)], axis=1)
            s_loc = _nt_dot(qm, k_loc) + bias
            s_ctx = jnp.dot(qm, kct, preferred_element_type=F32)
            m = jnp.maximum(jnp.max(s_loc, axis=-1, keepdims=True), jnp.max(s_ctx, axis=-1, keepdims=True))
            e_loc = jnp.exp(s_loc - m)
            e_ctx = jnp.exp(s_ctx - m)
            l = jnp.sum(e_loc, axis=-1, keepdims=True) + jnp.sum(e_ctx, axis=-1, keepdims=True)
            o = (jnp.dot(e_loc.astype(BF16), v_loc, preferred_element_type=F32)
                 + _nt_dot(e_ctx.astype(BF16), vct))
            outs.append(o / l)
        bias_at += hi - lo
        o_ref[qb * ATT_BLOCK:(qb + 1) * ATT_BLOCK, :] = jnp.where(first, outs[0], outs[1]).astype(BF16)


def _latent_attn(rpb_flat, q, k, v, cache_kt, cache_vt, *, layer_i, n_prompt, latent_seq):
    d = q.shape[1]
    n_batch, _, _, past = cache_kt.shape
    rows = latent_seq // GRID_W
    plan = _latent_block_plan(rows)
    n_bias = sum(hi - lo for lo, hi in plan)
    first_latent = n_prompt // latent_seq
    tok_q = pl.BlockSpec((latent_seq, LANE), lambda p, b: (first_latent + b, p))
    tok = pl.BlockSpec((latent_seq, LANE), lambda p, b: (b, p))
    ctx = pl.BlockSpec((None, None, LANE, past), lambda p, b: (b, layer_i, p, 0))
    return pl.pallas_call(
        functools.partial(_latent_attn_kernel, plan=plan, rows=rows),
        grid=(d // LANE, n_batch),
        in_specs=[pl.BlockSpec(memory_space=pltpu.SMEM), tok_q, tok, tok, ctx, ctx],
        out_specs=pl.BlockSpec((latent_seq, LANE), lambda p, b: (b, p)),
        out_shape=jax.ShapeDtypeStruct((n_batch * latent_seq, d), BF16),
        scratch_shapes=[pltpu.VMEM((N_DROW, GRID_W, LANE), F32),
                        pltpu.VMEM((2, n_bias, ATT_BLOCK, ATT_BLOCK), F32)],
        compiler_params=pltpu.CompilerParams(
            dimension_semantics=("arbitrary", "arbitrary"), vmem_limit_bytes=VMEM_LIMIT),
        name="latent_attn",
    )(rpb_flat, q, k, v, cache_kt, cache_vt)


def _conv_kernel(x_ref, mod_ref, g_ref, w1_ref, wdw_ref, bdw_ref, lng_ref, lnb_ref, a_ref, pad_ref,
                 *, n_prompt_tiles, prompt_seq, rows_per_step):
    tm, d = x_ref.shape
    h = _rms_mod(x_ref[...], g_ref[...], mod_ref[0:1, :], mod_ref[1:2, :])
    ag = jnp.dot(h.astype(BF16), w1_ref[...], preferred_element_type=F32)
    u = ag[:, :d] * jax.nn.sigmoid(ag[:, d:])
    first_tap = CONV_PAD - CONV_WIDTH // 2
    rs = rows_per_step

    def conv(seq):
        stride = seq + 2 * CONV_PAD
        for s in range(tm // seq):
            base = s * stride
            pad_ref[base:base + CONV_PAD, :] = jnp.zeros((CONV_PAD, d), F32)
            pad_ref[base + CONV_PAD:base + CONV_PAD + seq, :] = u[s * seq:(s + 1) * seq, :]
            pad_ref[base + CONV_PAD + seq:base + stride, :] = jnp.zeros((CONV_PAD, d), F32)
        for s in range(tm // seq):
            base = s * stride

            def step(ci, carry, base=base, s=s):
                r0 = pl.multiple_of(ci * rs, rs)
                parts = []
                for lc in range(d // LANE):
                    ls = slice(lc * LANE, (lc + 1) * LANE)
                    win = pad_ref[pl.ds(base + r0, rs + 2 * CONV_PAD), ls]
                    acc = jnp.broadcast_to(bdw_ref[:, ls], (rs, LANE))
                    for b in range(SUBLANE):
                        vb = None
                        for o in range(b, CONV_WIDTH + first_tap, SUBLANE):
                            t = o - first_tap
                            if t < 0:
                                continue
                            term = win[o - b:o - b + rs + SUBLANE, :] * wdw_ref[t:t + 1, ls]
                            vb = term if vb is None else vb + term
                        if b:
                            vb = pltpu.roll(vb, rs + SUBLANE - b, axis=0)
                        acc = acc + vb[:rs, :]
                    parts.append(acc)
                acc = jnp.concatenate(parts, axis=1)
                mu = jnp.mean(acc, axis=-1, keepdims=True)
                xc = acc - mu
                y = xc * lax.rsqrt(jnp.mean(xc * xc, axis=-1, keepdims=True) + LN_EPS)
                y = y * lng_ref[...] + lnb_ref[...]
                a_ref[pl.ds(s * seq + r0, rs), :] = (y * jax.nn.sigmoid(y)).astype(BF16)
                return carry

            lax.fori_loop(0, seq // rs, step, 0)

    is_prompt = pl.program_id(0) < n_prompt_tiles
    pl.when(is_prompt)(lambda: conv(prompt_seq))
    pl.when(jnp.logical_not(is_prompt))(lambda: conv(tm))


def _conv_front(x, mod_l, g, w_pw1, w_dw, b_dw, ln_g, ln_b, *, n_prompt, prompt_seq, latent_seq):
    n_tok, d = x.shape
    tm = latent_seq
    npt = n_prompt // tm
    row = functools.partial(_cond_row, n_prompt_tiles=npt, tiles_per_latent_seq=1)
    tok = pl.BlockSpec((tm, d), lambda i: (i, 0))
    pad_rows = (tm // prompt_seq) * (prompt_seq + 2 * CONV_PAD)
    return pl.pallas_call(
        functools.partial(_conv_kernel, n_prompt_tiles=npt, prompt_seq=prompt_seq, rows_per_step=64),
        grid=(n_tok // tm,),
        in_specs=[
            tok,
            pl.BlockSpec((None, 6, d), lambda i: (row(i), 0, 0)),
            _const_spec((1, d)),
            _const_spec((d, 2 * d)),
            _const_spec((CONV_WIDTH, d)),
            _const_spec((1, d)),
            _const_spec((1, d)),
            _const_spec((1, d)),
        ],
        out_specs=tok,
        out_shape=jax.ShapeDtypeStruct((n_tok, d), BF16),
        scratch_shapes=[pltpu.VMEM((pad_rows, d), F32)],
        compiler_params=pltpu.CompilerParams(
            dimension_semantics=("arbitrary",), vmem_limit_bytes=VMEM_LIMIT),
        name="conv_front",
    )(x, mod_l, g, w_pw1, w_dw, b_dw, ln_g, ln_b)


def _post_kernel(*refs, n_x, n_a, n_out, n_prompt_tiles, ff_chunk, final):
    x_refs, a_refs = refs[:n_x], refs[n_x:n_x + n_a]
    mod_ref, g_ref, wp_ref, wup_ref, wdn_ref, fg_ref = refs[n_x + n_a:n_x + n_a + 6]
    o_refs = refs[n_x + n_a + 6:]
    is_prompt = pl.program_id(0) < n_prompt_tiles
    y = jnp.dot(_token_tile(a_refs, is_prompt), wp_ref[...], preferred_element_type=F32)
    x1 = _token_tile(x_refs, is_prompt) + mod_ref[2:3, :] * y
    h = _rms_mod(x1, g_ref[...], mod_ref[3:4, :], mod_ref[4:5, :]).astype(BF16)
    acc = jnp.zeros_like(x1)
    for c in range(wup_ref.shape[1] // ff_chunk):
        sl = slice(c * ff_chunk, (c + 1) * ff_chunk)
        u = jnp.maximum(jnp.dot(h, wup_ref[:, sl], preferred_element_type=F32), 0.0)
        acc = acc + jnp.dot((u * u).astype(BF16), wdn_ref[sl, :], preferred_element_type=F32)
    out = x1 + mod_ref[5:6, :] * acc
    if final:
        out = out * lax.rsqrt(jnp.mean(out * out, axis=-1, keepdims=True) + RMS_EPS) * fg_ref[...]
    if n_out == 1:
        o_refs[0][...] = out
    else:
        @pl.when(is_prompt)
        def _():
            o_refs[0][...] = out

        @pl.when(jnp.logical_not(is_prompt))
        def _():
            o_refs[1][...] = out


def _post(x_parts, a_parts, mod_l, g, w_proj, w_up, w_down, final_g, *, n_prompt, n_tok, latent_seq, tm, final):
    d = x_parts[0].shape[1]
    d_ff = w_up.shape[1]
    npt = n_prompt // tm
    row = functools.partial(_cond_row, n_prompt_tiles=npt, tiles_per_latent_seq=latent_seq // tm)
    if final:
        out_shape = [jax.ShapeDtypeStruct((n_prompt, d), F32), jax.ShapeDtypeStruct((n_tok - n_prompt, d), F32)]
    else:
        out_shape = [jax.ShapeDtypeStruct((n_tok, d), F32)]
    return pl.pallas_call(
        functools.partial(_post_kernel, n_x=len(x_parts), n_a=len(a_parts), n_out=len(out_shape),
                          n_prompt_tiles=npt, ff_chunk=1024, final=final),
        grid=(n_tok // tm,),
        in_specs=_token_specs(x_parts, tm, npt) + _token_specs(a_parts, tm, npt) + [
            pl.BlockSpec((None, 6, d), lambda i: (row(i), 0, 0)),
            _const_spec((1, d)),
            _const_spec((d, d)),
            _const_spec((d, d_ff)),
            _const_spec((d_ff, d)),
            _const_spec((1, d)),
        ],
        out_specs=_token_specs(out_shape, tm, npt),
        out_shape=out_shape,
        compiler_params=pltpu.CompilerParams(
            dimension_semantics=("arbitrary",), vmem_limit_bytes=VMEM_LIMIT),
        name="post_mlp",
    )(*x_parts, *a_parts, mod_l, g, w_proj, w_up, w_down, final_g)


def kernel(x_prompt, x_sample, cache_k, cache_v, c, c_ctx, norm_g, w_ada, b_ada, w_qkv, w_o, rpb,
           w_pw1, w_dw, b_dw, conv_ln_g, conv_ln_b, w_pw2, w_up, w_down, final_g):
    batch, seq, d = x_prompt.shape
    dec_batch, dec_seq, _ = x_sample.shape
    depth = w_ada.shape[0]
    n_attn = w_qkv.shape[0]
    past = cache_k.shape[2]
    n_prompt = batch * seq
    n_tok = n_prompt + dec_batch * dec_seq
    head_dim = d // N_HEADS
    assert d == N_HEADS * head_dim and 2 * head_dim == LANE and LANE == 2 * GRID_W
    assert seq == ATT_BLOCK and past == ATT_BLOCK and dec_seq % ATT_BLOCK == 0
    assert rpb.shape[1:] == (N_HEADS, N_DROW, N_DCOL)
    assert 1 + dec_batch <= N_COND_ROWS
    tm = 512
    dims = dict(n_prompt=n_prompt, n_tok=n_tok, latent_seq=dec_seq)

    x_parts = [x_prompt.reshape(n_prompt, d), x_sample.reshape(dec_batch * dec_seq, d)]
    cond = jnp.concatenate([c_ctx[None, :], c, jnp.zeros((N_COND_ROWS - 1 - dec_batch, d), F32)], axis=0)
    mod = _adaln(cond, w_ada, b_ada).reshape(depth, N_COND_ROWS, 6, d)
    ckt = jnp.transpose(cache_k, (0, 1, 3, 4, 2)).reshape(dec_batch, n_attn, d, past)
    cvt = jnp.transpose(cache_v, (0, 1, 3, 4, 2)).reshape(dec_batch, n_attn, d, past)
    final_g2 = final_g.reshape(1, d)

    new_caches = ()
    for l in range(depth):
        i = l // 2
        mod_l = mod[l]
        g1 = norm_g[l, 0].reshape(1, d)
        g2 = norm_g[l, 1].reshape(1, d)
        if l % 2 == 0:
            w_kv_t = jnp.transpose(w_qkv[i][:, d:]).astype(BF16)
            q, k, v, *new_caches = _qkv(x_parts, mod_l, g1, w_qkv[i].astype(BF16), w_kv_t, new_caches, layer_i=i,
                                        n_attn=n_attn, seq=seq, tm=tm, scale=head_dim ** -0.5, **dims)
            o_p = _prompt_attn(q, *new_caches, layer_i=i, n_prompt=n_prompt, seq=seq)
            o_s = _latent_attn(rpb[i].reshape(-1), q, k, v, ckt, cvt, layer_i=i, n_prompt=n_prompt,
                               latent_seq=dec_seq)
            a_parts = [o_p, o_s]
            w_proj = w_o[i]
        else:
            assert len(x_parts) == 1
            a_parts = [_conv_front(x_parts[0], mod_l, g1, w_pw1[i].astype(BF16), w_dw[i], b_dw[i].reshape(1, d),
                                   conv_ln_g[i].reshape(1, d), conv_ln_b[i].reshape(1, d), n_prompt=n_prompt,
                                   prompt_seq=seq, latent_seq=dec_seq)]
            w_proj = w_pw2[i]
        x_parts = _post(x_parts, a_parts, mod_l, g2, w_proj.astype(BF16), w_up[l].astype(BF16),
                        w_down[l].astype(BF16), final_g2, tm=tm, final=(l == depth - 1), **dims)

    y_prompt, y_sample = x_parts
    new_k, new_v = [jnp.transpose(t.reshape(batch, n_attn, N_HEADS, head_dim, seq), (0, 1, 4, 2, 3))
                    for t in new_caches]
    return (y_prompt.reshape(batch, seq, d), y_sample.reshape(dec_batch, dec_seq, d), new_k, new_v)
```

```python
import functools

import numpy as np
import jax
import jax.numpy as jnp
from jax import lax
from jax.experimental import pallas as pl
from jax.experimental.pallas import tpu as pltpu

F32 = jnp.float32
BF16 = jnp.bfloat16

RMS_EPS = 1e-6
LN_EPS = 1e-5
N_HEADS = 16
GRID_W = 64
WIN_ROWS = 8
WIN_COLS = 16
N_DROW = 2 * WIN_ROWS - 1
N_DCOL = 2 * WIN_COLS - 1
CONV_WIDTH = 31
CONV_PAD = 16
N_COND_ROWS = 8

LANE = 128
SUBLANE = 8
ATT_BLOCK = 256
ROWS_PER_BLOCK = ATT_BLOCK // GRID_W
VMEM_LIMIT = 56 * 1024 * 1024


def _rms_mod(xf, g, shift, scale):
    y = xf * lax.rsqrt(jnp.mean(xf * xf, axis=-1, keepdims=True) + RMS_EPS)
    return (y * g) * (1.0 + scale) + shift


def _cond_row(i, n_prompt_tiles, tiles_per_latent_seq):
    return jnp.where(i < n_prompt_tiles, 0, 1 + (i - n_prompt_tiles) // tiles_per_latent_seq)


def _const_spec(shape):
    nd = len(shape)
    return pl.BlockSpec(shape, lambda *_: (0,) * nd, pipeline_mode=pl.Buffered(1))


def _layer_spec(shape, layer):
    nd = len(shape)
    return pl.BlockSpec((None,) + tuple(shape), lambda *_: (layer,) + (0,) * nd, pipeline_mode=pl.Buffered(1))


def _token_specs(parts, tm, n_prompt_tiles):
    d = parts[0].shape[1]
    if len(parts) == 1:
        return [pl.BlockSpec((tm, d), lambda i: (i, 0))]
    return [pl.BlockSpec((tm, d), lambda i: (jnp.minimum(i, n_prompt_tiles - 1), 0)),
            pl.BlockSpec((tm, d), lambda i: (jnp.maximum(i - n_prompt_tiles, 0), 0))]


def _token_tile(refs, is_prompt):
    if len(refs) == 1:
        return refs[0][...]
    return jnp.where(is_prompt, refs[0][...], refs[1][...])


def _adaln_kernel(cond_ref, w_ref, b_ref, o_ref):
    s = cond_ref[...]
    s = s * jax.nn.sigmoid(s)
    o_ref[0] = jnp.dot(s.astype(BF16), w_ref[0].astype(BF16), preferred_element_type=F32) + b_ref[0]


def _adaln(cond, w_ada, b_ada):
    depth, d, n6 = w_ada.shape
    tn = n6 // 4
    return pl.pallas_call(
        _adaln_kernel,
        grid=(depth, n6 // tn),
        in_specs=[
            pl.BlockSpec((N_COND_ROWS, d), lambda l, j: (0, 0)),
            pl.BlockSpec((1, d, tn), lambda l, j: (l, 0, j)),
            pl.BlockSpec((1, 1, tn), lambda l, j: (l, 0, j)),
        ],
        out_specs=pl.BlockSpec((1, N_COND_ROWS, tn), lambda l, j: (l, 0, j)),
        out_shape=jax.ShapeDtypeStruct((depth, N_COND_ROWS, n6), F32),
        compiler_params=pltpu.CompilerParams(
            dimension_semantics=("arbitrary", "arbitrary"), vmem_limit_bytes=VMEM_LIMIT),
        name="adaln",
    )(cond, w_ada, b_ada.reshape(depth, 1, n6))


def _qkv_kernel(*refs, n_x, n_prev, layer_i, n_prompt_tiles, scale):
    x_refs = refs[:n_x]
    mod_ref, g_ref, wq_ref, wt_ref = refs[n_x:n_x + 4]
    q_ref, ktl_ref, vtl_ref, kt_ref, vt_ref = refs[n_x + 4 + n_prev:]
    d = q_ref.shape[1]
    n_seq, seq = kt_ref.shape[0], kt_ref.shape[-1]
    is_prompt = pl.program_id(0) < n_prompt_tiles
    h = _rms_mod(_token_tile(x_refs, is_prompt), g_ref[...], mod_ref[0:1, :], mod_ref[1:2, :]).astype(BF16)
    q_ref[...] = (jnp.dot(h, wq_ref[...], preferred_element_type=F32) * scale).astype(BF16)

    @pl.when(is_prompt)
    def _():
        for s in range(n_seq):
            hs = h[s * seq:(s + 1) * seq, :]
            for t_ref, lo in ((kt_ref, 0), (vt_ref, d)):
                t = _nt_dot(wt_ref[lo:lo + d, :], hs)
                if n_prev:
                    t_ref[s] = t
                else:
                    for j in range(t_ref.shape[1]):
                        t_ref[s, j] = t if j == layer_i else jnp.zeros_like(t)

    @pl.when(jnp.logical_not(is_prompt))
    def _():
        ktl_ref[...] = _nt_dot(wt_ref[:d, :], h).astype(BF16)
        vtl_ref[...] = _nt_dot(wt_ref[d:, :], h).astype(BF16)


def _qkv(x_parts, mod_l, g, w_qkv, w_kv_t, prev_caches, *, layer_i, seq, n_prompt, n_tok, latent_seq, tm, scale):
    d = x_parts[0].shape[1]
    n_attn = w_qkv.shape[0]
    npt = n_prompt // tm
    tiles_per_seq = latent_seq // tm
    row = functools.partial(_cond_row, n_prompt_tiles=npt, tiles_per_latent_seq=tiles_per_seq)
    tok = pl.BlockSpec((tm, d), lambda i: (i, 0))
    latent_t = pl.BlockSpec((None, d, tm), lambda i: (jnp.maximum(i - npt, 0) // tiles_per_seq, 0,
                                                     jnp.maximum(i - npt, 0) % tiles_per_seq))
    if prev_caches:
        cache_blk = pl.BlockSpec((tm // seq, None, d, seq), lambda i: (jnp.minimum(i, npt - 1), layer_i, 0, 0))
    else:
        cache_blk = pl.BlockSpec((tm // seq, n_attn, d, seq), lambda i: (jnp.minimum(i, npt - 1), 0, 0, 0))
    cache_shape = jax.ShapeDtypeStruct((n_prompt // seq, n_attn, d, seq), F32)
    latent_shape = jax.ShapeDtypeStruct(((n_tok - n_prompt) // latent_seq, d, latent_seq), BF16)
    n_in = len(x_parts) + 4
    return pl.pallas_call(
        functools.partial(_qkv_kernel, n_x=len(x_parts), n_prev=len(prev_caches), layer_i=layer_i,
                          n_prompt_tiles=npt, scale=scale),
        grid=(n_tok // tm,),
        in_specs=_token_specs(x_parts, tm, npt) + [
            pl.BlockSpec((None, 6, d), lambda i: (row(i), 0, 0)),
            _const_spec((1, d)),
            _layer_spec((d, d), layer_i),
            _layer_spec((2 * d, d), layer_i),
        ] + [pl.BlockSpec(memory_space=pl.ANY)] * len(prev_caches),
        out_specs=[tok, latent_t, latent_t, cache_blk, cache_blk],
        out_shape=[jax.ShapeDtypeStruct((n_tok, d), BF16), latent_shape, latent_shape, cache_shape, cache_shape],
        input_output_aliases={n_in + j: 3 + j for j in range(len(prev_caches))},
        compiler_params=pltpu.CompilerParams(
            dimension_semantics=("arbitrary",), vmem_limit_bytes=VMEM_LIMIT),
        name="qkv",
    )(*x_parts, mod_l, g, w_qkv, w_kv_t, *prev_caches)


def _nt_dot(a, b):
    return lax.dot_general(a, b, (((1,), (1,)), ((), ())), preferred_element_type=F32)


def _pair_heads(qp, first):
    zero = jnp.zeros_like(qp)
    return jnp.where(first, qp, zero), jnp.where(first, zero, qp)


def _prompt_attn_kernel(q_ref, kt_ref, vt_ref, o_ref):
    seq, d = q_ref.shape
    half = LANE // 2
    first = lax.broadcasted_iota(jnp.int32, (seq, LANE), 1) < half
    ones = jnp.ones((2 * SUBLANE, seq), BF16)
    for p in range(d // LANE):
        sl = slice(p * LANE, (p + 1) * LANE)
        ktp = kt_ref[sl, :].astype(BF16)
        vtp = jnp.concatenate([vt_ref[sl, :].astype(BF16), ones], axis=0)
        outs = []
        for qm in _pair_heads(q_ref[:, sl], first):
            s = jnp.dot(qm, ktp, preferred_element_type=F32)
            e = jnp.exp(s - jnp.max(s, axis=-1, keepdims=True))
            ot = _nt_dot(vtp, e.astype(BF16))
            outs.append(ot[:LANE, :] / ot[LANE:LANE + 1, :])
        ot = jnp.concatenate([outs[0][:half, :], outs[1][half:, :]], axis=0)
        o_ref[:, sl] = ot.T.astype(BF16)


def _prompt_attn(q, cache_kt, cache_vt, *, layer_i, n_prompt, seq):
    d = q.shape[1]
    tok = pl.BlockSpec((seq, d), lambda b: (b, 0))
    feat = pl.BlockSpec((None, None, d, seq), lambda b: (b, layer_i, 0, 0))
    return pl.pallas_call(
        _prompt_attn_kernel,
        grid=(n_prompt // seq,),
        in_specs=[tok, feat, feat],
        out_specs=tok,
        out_shape=jax.ShapeDtypeStruct((n_prompt, d), BF16),
        compiler_params=pltpu.CompilerParams(
            dimension_semantics=("arbitrary",), vmem_limit_bytes=VMEM_LIMIT),
        name="prompt_attn",
    )(q, cache_kt, cache_vt)


def _row_start(r, rows):
    kr = min(WIN_ROWS, rows)
    return min(max(r - kr // 2, 0), rows - kr)


def _latent_block_plan(rows):
    kr = min(WIN_ROWS, rows)
    plan = []
    for qb in range(rows // ROWS_PER_BLOCK):
        starts = [_row_start(r, rows) for r in range(qb * ROWS_PER_BLOCK, (qb + 1) * ROWS_PER_BLOCK)]
        plan.append((min(starts) // ROWS_PER_BLOCK, (max(starts) + kr - 1) // ROWS_PER_BLOCK + 1))
    return plan


def _build_pair_bias(rpb_ref, tile_ref, bias_ref, head0, *, plan, rows):
    kr = min(WIN_ROWS, rows)
    shape = (GRID_W, LANE)
    qc = lax.broadcasted_iota(jnp.int32, shape, 0)
    lane = lax.broadcasted_iota(jnp.int32, shape, 1)
    kc = lane & (GRID_W - 1)
    delta = kc - qc + (WIN_COLS - 1)
    col_start = jnp.clip(qc - WIN_COLS // 2, 0, GRID_W - WIN_COLS)
    in_window = (kc >= col_start) & (kc < col_start + WIN_COLS)
    left = lane < GRID_W
    neg = jnp.full(shape, -jnp.inf, F32)
    for hh in range(2):
        base = (head0 + hh) * (N_DROW * N_DCOL)
        for dr in range(N_DROW):
            w = neg
            for dc in range(N_DCOL):
                w = jnp.where(delta == dc, rpb_ref[base + dr * N_DCOL + dc], w)
            tile_ref[dr] = jnp.where(in_window, w, neg)
        t = 0
        for qb, (lo, hi) in enumerate(plan):
            for kb in range(lo, hi):
                for a in range(ROWS_PER_BLOCK):
                    r = qb * ROWS_PER_BLOCK + a
                    start = _row_start(r, rows)
                    for c in range(ATT_BLOCK // LANE):
                        halves = []
                        for jj in range(LANE // GRID_W):
                            k_row = kb * ROWS_PER_BLOCK + c * (LANE // GRID_W) + jj
                            ok = start <= k_row < start + kr
                            halves.append(tile_ref[k_row - r + WIN_ROWS - 1] if ok else None)
                        if halves[0] is None and halves[1] is None:
                            val = neg
                        else:
                            val = jnp.where(left, neg if halves[0] is None else halves[0],
                                            neg if halves[1] is None else halves[1])
                        bias_ref[hh, t, a * GRID_W:(a + 1) * GRID_W, c * LANE:(c + 1) * LANE] = val
                t += 1


def _latent_attn_kernel(rpb_ref, q_ref, kt_ref, vt_ref, kct_ref, vct_ref, o_ref, tile_ref, bias_ref, *, plan, rows):
    @pl.when(pl.program_id(1) == 0)
    def _():
        _build_pair_bias(rpb_ref, tile_ref, bias_ref, 2 * pl.program_id(0), plan=plan, rows=rows)

    half = LANE // 2
    first = lax.broadcasted_iota(jnp.int32, (ATT_BLOCK, LANE), 1) < half
    kct = kct_ref[...].astype(BF16)
    vct = jnp.concatenate([vct_ref[...].astype(BF16), jnp.ones((2 * SUBLANE, kct.shape[1]), BF16)], axis=0)
    vt = jnp.concatenate([vt_ref[...], jnp.ones((2 * SUBLANE, vt_ref.shape[1]), BF16)], axis=0)
    bias_at = 0
    for qb, (lo, hi) in enumerate(plan):
        keys = slice(lo * ATT_BLOCK, hi * ATT_BLOCK)
        outs = []
        for hh, qm in enumerate(_pair_heads(q_ref[qb * ATT_BLOCK:(qb + 1) * ATT_BLOCK, :], first)):
            bias = jnp.concatenate([bias_ref[hh, bias_at + j] for j in range(hi - lo)], axis=1)
            s_loc = jnp.dot(qm, kt_ref[:, keys], preferred_element_type=F32) + bias
            s_ctx = jnp.dot(qm, kct, preferred_element_type=F32)
            m = jnp.maximum(jnp.max(s_loc, axis=-1, keepdims=True), jnp.max(s_ctx, axis=-1, keepdims=True))
            e_loc = jnp.exp(s_loc - m).astype(BF16)
            e_ctx = jnp.exp(s_ctx - m).astype(BF16)
            ot = _nt_dot(vt[:, keys], e_loc) + _nt_dot(vct, e_ctx)
            outs.append(ot[:LANE, :] / ot[LANE:LANE + 1, :])
        bias_at += hi - lo
        ot = jnp.concatenate([outs[0][:half, :], outs[1][half:, :]], axis=0)
        o_ref[qb * ATT_BLOCK:(qb + 1) * ATT_BLOCK, :] = ot.T.astype(BF16)


def _latent_attn(rpb_flat, q, kt, vt, cache_kt, cache_vt, *, layer_i, n_prompt, latent_seq):
    d = q.shape[1]
    n_batch, _, _, past = cache_kt.shape
    rows = latent_seq // GRID_W
    plan = _latent_block_plan(rows)
    n_bias = sum(hi - lo for lo, hi in plan)
    first_latent = n_prompt // latent_seq
    tok_q = pl.BlockSpec((latent_seq, LANE), lambda p, b: (first_latent + b, p))
    feat = pl.BlockSpec((None, LANE, latent_seq), lambda p, b: (b, p, 0))
    ctx = pl.BlockSpec((None, None, LANE, past), lambda p, b: (b, layer_i, p, 0))
    return pl.pallas_call(
        functools.partial(_latent_attn_kernel, plan=plan, rows=rows),
        grid=(d // LANE, n_batch),
        in_specs=[pl.BlockSpec(memory_space=pltpu.SMEM), tok_q, feat, feat, ctx, ctx],
        out_specs=pl.BlockSpec((latent_seq, LANE), lambda p, b: (b, p)),
        out_shape=jax.ShapeDtypeStruct((n_batch * latent_seq, d), BF16),
        scratch_shapes=[pltpu.VMEM((N_DROW, GRID_W, LANE), F32),
                        pltpu.VMEM((2, n_bias, ATT_BLOCK, ATT_BLOCK), F32)],
        compiler_params=pltpu.CompilerParams(
            dimension_semantics=("arbitrary", "arbitrary"), vmem_limit_bytes=VMEM_LIMIT),
        name="latent_attn",
    )(rpb_flat, q, kt, vt, cache_kt, cache_vt)


def _conv_kernel(x_ref, mod_ref, g_ref, w1_ref, wdw_ref, bdw_ref, lng_ref, lnb_ref, a_ref, pad_ref,
                 *, n_prompt_tiles, prompt_seq, rows_per_step):
    tm, d = x_ref.shape
    h = _rms_mod(x_ref[...], g_ref[...], mod_ref[0:1, :], mod_ref[1:2, :])
    ag = jnp.dot(h.astype(BF16), w1_ref[...], preferred_element_type=F32)
    u = ag[:, :d] * jax.nn.sigmoid(ag[:, d:])
    first_tap = CONV_PAD - CONV_WIDTH // 2
    rs = rows_per_step

    def conv(seq):
        stride = seq + 2 * CONV_PAD
        for s in range(tm // seq):
            base = s * stride
            pad_ref[base:base + CONV_PAD, :] = jnp.zeros((CONV_PAD, d), F32)
            pad_ref[base + CONV_PAD:base + CONV_PAD + seq, :] = u[s * seq:(s + 1) * seq, :]
            pad_ref[base + CONV_PAD + seq:base + stride, :] = jnp.zeros((CONV_PAD, d), F32)
        for s in range(tm // seq):
            base = s * stride

            def step(ci, carry, base=base, s=s):
                r0 = pl.multiple_of(ci * rs, rs)
                parts = []
                for lc in range(d // LANE):
                    ls = slice(lc * LANE, (lc + 1) * LANE)
                    win = pad_ref[pl.ds(base + r0, rs + 2 * CONV_PAD), ls]
                    acc = jnp.broadcast_to(bdw_ref[:, ls], (rs, LANE))
                    for b in range(SUBLANE):
                        vb = None
                        for o in range(b, CONV_WIDTH + first_tap, SUBLANE):
                            t = o - first_tap
                            if t < 0:
                                continue
                            term = win[o - b:o - b + rs + SUBLANE, :] * wdw_ref[t:t + 1, ls]
                            vb = term if vb is None else vb + term
                        if b:
                            vb = pltpu.roll(vb, rs + SUBLANE - b, axis=0)
                        acc = acc + vb[:rs, :]
                    parts.append(acc)
                acc = jnp.concatenate(parts, axis=1)
                mu = jnp.mean(acc, axis=-1, keepdims=True)
                xc = acc - mu
                y = xc * lax.rsqrt(jnp.mean(xc * xc, axis=-1, keepdims=True) + LN_EPS)
                y = y * lng_ref[...] + lnb_ref[...]
                a_ref[pl.ds(s * seq + r0, rs), :] = (y * jax.nn.sigmoid(y)).astype(BF16)
                return carry

            lax.fori_loop(0, seq // rs, step, 0)

    is_prompt = pl.program_id(0) < n_prompt_tiles
    pl.when(is_prompt)(lambda: conv(prompt_seq))
    pl.when(jnp.logical_not(is_prompt))(lambda: conv(tm))


def _conv_front(x, mod_l, g, w_pw1, w_dw, b_dw, ln_g, ln_b, *, layer_i, n_prompt, prompt_seq, latent_seq):
    n_tok, d = x.shape
    tm = latent_seq
    npt = n_prompt // tm
    row = functools.partial(_cond_row, n_prompt_tiles=npt, tiles_per_latent_seq=1)
    tok = pl.BlockSpec((tm, d), lambda i: (i, 0))
    pad_rows = (tm // prompt_seq) * (prompt_seq + 2 * CONV_PAD)
    return pl.pallas_call(
        functools.partial(_conv_kernel, n_prompt_tiles=npt, prompt_seq=prompt_seq, rows_per_step=64),
        grid=(n_tok // tm,),
        in_specs=[
            tok,
            pl.BlockSpec((None, 6, d), lambda i: (row(i), 0, 0)),
            _const_spec((1, d)),
            _layer_spec((d, 2 * d), layer_i),
            _const_spec((CONV_WIDTH, d)),
            _const_spec((1, d)),
            _const_spec((1, d)),
            _const_spec((1, d)),
        ],
        out_specs=tok,
        out_shape=jax.ShapeDtypeStruct((n_tok, d), BF16),
        scratch_shapes=[pltpu.VMEM((pad_rows, d), F32)],
        compiler_params=pltpu.CompilerParams(
            dimension_semantics=("arbitrary",), vmem_limit_bytes=VMEM_LIMIT),
        name="conv_front",
    )(x, mod_l, g, w_pw1, w_dw, b_dw, ln_g, ln_b)


def _post_kernel(*refs, n_x, n_a, n_out, n_prompt_tiles, ff_chunk, final):
    x_refs, a_refs = refs[:n_x], refs[n_x:n_x + n_a]
    mod_ref, g_ref, wp_ref, wup_ref, wdn_ref, fg_ref = refs[n_x + n_a:n_x + n_a + 6]
    o_refs = refs[n_x + n_a + 6:]
    is_prompt = pl.program_id(0) < n_prompt_tiles
    y = jnp.dot(_token_tile(a_refs, is_prompt), wp_ref[...], preferred_element_type=F32)
    x1 = _token_tile(x_refs, is_prompt) + mod_ref[2:3, :] * y
    h = _rms_mod(x1, g_ref[...], mod_ref[3:4, :], mod_ref[4:5, :]).astype(BF16)
    acc = jnp.zeros_like(x1)
    for c in range(wup_ref.shape[1] // ff_chunk):
        sl = slice(c * ff_chunk, (c + 1) * ff_chunk)
        u = jnp.maximum(jnp.dot(h, wup_ref[:, sl], preferred_element_type=F32), 0.0)
        acc = acc + jnp.dot((u * u).astype(BF16), wdn_ref[sl, :], preferred_element_type=F32)
    out = x1 + mod_ref[5:6, :] * acc
    if final:
        out = out * lax.rsqrt(jnp.mean(out * out, axis=-1, keepdims=True) + RMS_EPS) * fg_ref[...]
    if n_out == 1:
        o_refs[0][...] = out
    else:
        @pl.when(is_prompt)
        def _():
            o_refs[0][...] = out

        @pl.when(jnp.logical_not(is_prompt))
        def _():
            o_refs[1][...] = out


def _post(x_parts, a_parts, mod_l, g, w_proj, w_up, w_down, final_g, *, proj_i, layer, n_prompt, n_tok,
          latent_seq, tm, final):
    d = x_parts[0].shape[1]
    d_ff = w_up.shape[2]
    npt = n_prompt // tm
    row = functools.partial(_cond_row, n_prompt_tiles=npt, tiles_per_latent_seq=latent_seq // tm)
    if final:
        out_shape = [jax.ShapeDtypeStruct((n_prompt, d), F32), jax.ShapeDtypeStruct((n_tok - n_prompt, d), F32)]
    else:
        out_shape = [jax.ShapeDtypeStruct((n_tok, d), F32)]
    return pl.pallas_call(
        functools.partial(_post_kernel, n_x=len(x_parts), n_a=len(a_parts), n_out=len(out_shape),
                          n_prompt_tiles=npt, ff_chunk=1024, final=final),
        grid=(n_tok // tm,),
        in_specs=_token_specs(x_parts, tm, npt) + _token_specs(a_parts, tm, npt) + [
            pl.BlockSpec((None, 6, d), lambda i: (row(i), 0, 0)),
            _const_spec((1, d)),
            _layer_spec((d, d), proj_i),
            _layer_spec((d, d_ff), layer),
            _layer_spec((d_ff, d), layer),
            _const_spec((1, d)),
        ],
        out_specs=_token_specs(out_shape, tm, npt),
        out_shape=out_shape,
        compiler_params=pltpu.CompilerParams(
            dimension_semantics=("arbitrary",), vmem_limit_bytes=VMEM_LIMIT),
        name="post_mlp",
    )(*x_parts, *a_parts, mod_l, g, w_proj, w_up, w_down, final_g)


def kernel(x_prompt, x_sample, cache_k, cache_v, c, c_ctx, norm_g, w_ada, b_ada, w_qkv, w_o, rpb,
           w_pw1, w_dw, b_dw, conv_ln_g, conv_ln_b, w_pw2, w_up, w_down, final_g):
    batch, seq, d = x_prompt.shape
    dec_batch, dec_seq, _ = x_sample.shape
    depth = w_ada.shape[0]
    n_attn = w_qkv.shape[0]
    past = cache_k.shape[2]
    n_prompt = batch * seq
    n_tok = n_prompt + dec_batch * dec_seq
    head_dim = d // N_HEADS
    assert d == N_HEADS * head_dim and 2 * head_dim == LANE and LANE == 2 * GRID_W
    assert seq == ATT_BLOCK and past == ATT_BLOCK and dec_seq % ATT_BLOCK == 0
    assert rpb.shape[1:] == (N_HEADS, N_DROW, N_DCOL)
    assert 1 + dec_batch <= N_COND_ROWS
    tm = 512
    dims = dict(n_prompt=n_prompt, n_tok=n_tok, latent_seq=dec_seq)

    x_parts = [x_prompt.reshape(n_prompt, d), x_sample.reshape(dec_batch * dec_seq, d)]
    cond = jnp.concatenate([c_ctx[None, :], c, jnp.zeros((N_COND_ROWS - 1 - dec_batch, d), F32)], axis=0)
    mod = _adaln(cond, w_ada, b_ada).reshape(depth, N_COND_ROWS, 6, d)
    ckt = jnp.transpose(cache_k, (0, 1, 3, 4, 2)).reshape(dec_batch, n_attn, d, past)
    cvt = jnp.transpose(cache_v, (0, 1, 3, 4, 2)).reshape(dec_batch, n_attn, d, past)
    final_g2 = final_g.reshape(1, d)
    w_qkv_b, w_o_b, w_pw1_b, w_pw2_b, w_up_b, w_down_b = [
        w.astype(BF16) for w in (w_qkv, w_o, w_pw1, w_pw2, w_up, w_down)]
    w_kv_t = jnp.transpose(w_qkv[:, :, d:], (0, 2, 1)).astype(BF16)

    new_caches = ()
    for l in range(depth):
        i = l // 2
        mod_l = mod[l]
        g1 = norm_g[l, 0].reshape(1, d)
        g2 = norm_g[l, 1].reshape(1, d)
        if l % 2 == 0:
            q, kt, vt, *new_caches = _qkv(x_parts, mod_l, g1, w_qkv_b, w_kv_t, new_caches, layer_i=i, seq=seq,
                                          tm=tm, scale=head_dim ** -0.5, **dims)
            o_p = _prompt_attn(q, *new_caches, layer_i=i, n_prompt=n_prompt, seq=seq)
            o_s = _latent_attn(rpb[i].reshape(-1), q, kt, vt, ckt, cvt, layer_i=i, n_prompt=n_prompt,
                               latent_seq=dec_seq)
            a_parts = [o_p, o_s]
            w_proj = w_o_b
        else:
            assert len(x_parts) == 1
            a_parts = [_conv_front(x_parts[0], mod_l, g1, w_pw1_b, w_dw[i], b_dw[i].reshape(1, d),
                                   conv_ln_g[i].reshape(1, d), conv_ln_b[i].reshape(1, d), layer_i=i,
                                   n_prompt=n_prompt, prompt_seq=seq, latent_seq=dec_seq)]
            w_proj = w_pw2_b
        x_parts = _post(x_parts, a_parts, mod_l, g2, w_proj, w_up_b, w_down_b, final_g2, proj_i=i, layer=l,
                        tm=tm, final=(l == depth - 1), **dims)

    y_prompt, y_sample = x_parts
    new_k, new_v = [jnp.transpose(t.reshape(batch, n_attn, N_HEADS, head_dim, seq), (0, 1, 4, 2, 3))
                    for t in new_caches]
    return (y_prompt.reshape(batch, seq, d), y_sample.reshape(dec_batch, dec_seq, d), new_k, new_v)
```

```python
import functools

import numpy as np
import jax
import jax.numpy as jnp
from jax import lax
from jax.experimental import pallas as pl
from jax.experimental.pallas import tpu as pltpu

F32 = jnp.float32
BF16 = jnp.bfloat16

RMS_EPS = 1e-6
LN_EPS = 1e-5
N_HEADS = 16
GRID_W = 64
WIN_ROWS = 8
WIN_COLS = 16
N_DROW = 2 * WIN_ROWS - 1
N_DCOL = 2 * WIN_COLS - 1
CONV_WIDTH = 31
CONV_PAD = 16
N_COND_ROWS = 8

LANE = 128
SUBLANE = 8
ATT_BLOCK = 256
ROWS_PER_BLOCK = ATT_BLOCK // GRID_W
VMEM_LIMIT = 56 * 1024 * 1024


def _rms_mod(xf, g, shift, scale):
    y = xf * lax.rsqrt(jnp.mean(xf * xf, axis=-1, keepdims=True) + RMS_EPS)
    return (y * g) * (1.0 + scale) + shift


def _cond_row(i, n_prompt_tiles, tiles_per_latent_seq):
    return jnp.where(i < n_prompt_tiles, 0, 1 + (i - n_prompt_tiles) // tiles_per_latent_seq)


def _const_spec(shape):
    nd = len(shape)
    return pl.BlockSpec(shape, lambda *_: (0,) * nd, pipeline_mode=pl.Buffered(1))


def _layer_spec(shape, layer):
    nd = len(shape)
    return pl.BlockSpec((None,) + tuple(shape), lambda *_: (layer,) + (0,) * nd, pipeline_mode=pl.Buffered(1))


def _token_specs(parts, tm, n_prompt_tiles):
    d = parts[0].shape[1]
    if len(parts) == 1:
        return [pl.BlockSpec((tm, d), lambda i: (i, 0))]
    return [pl.BlockSpec((tm, d), lambda i: (jnp.minimum(i, n_prompt_tiles - 1), 0)),
            pl.BlockSpec((tm, d), lambda i: (jnp.maximum(i - n_prompt_tiles, 0), 0))]


def _token_tile(refs, is_prompt):
    if len(refs) == 1:
        return refs[0][...]
    return jnp.where(is_prompt, refs[0][...], refs[1][...])


def _cast_specs(weights, layer, n_chunks, step_of):
    in_specs, out_specs, out_shapes = [], [], []
    for w in weights:
        _, r, c = w.shape
        chunk = lambda *idx: jnp.minimum(step_of(*idx), n_chunks - 1)
        in_specs.append(pl.BlockSpec((None, r // n_chunks, c), lambda *idx, chunk=chunk: (layer, chunk(*idx), 0)))
        out_specs.append(pl.BlockSpec((r // n_chunks, c), lambda *idx, chunk=chunk: (chunk(*idx), 0)))
        out_shapes.append(jax.ShapeDtypeStruct((r, c), BF16))
    return in_specs, out_specs, out_shapes


def _cast_chunks(step, n_chunks, w_refs, o_refs):
    @pl.when(step < n_chunks)
    def _():
        for w_ref, o_ref in zip(w_refs, o_refs):
            o_ref[...] = w_ref[...].astype(BF16)


def _adaln_kernel(cond_ref, w_ref, b_ref, o_ref):
    s = cond_ref[...]
    s = s * jax.nn.sigmoid(s)
    o_ref[0] = jnp.dot(s.astype(BF16), w_ref[0].astype(BF16), preferred_element_type=F32) + b_ref[0]


def _adaln(cond, w_ada, b_ada):
    depth, d, n6 = w_ada.shape
    tn = n6 // 4
    return pl.pallas_call(
        _adaln_kernel,
        grid=(depth, n6 // tn),
        in_specs=[
            pl.BlockSpec((N_COND_ROWS, d), lambda l, j: (0, 0)),
            pl.BlockSpec((1, d, tn), lambda l, j: (l, 0, j)),
            pl.BlockSpec((1, 1, tn), lambda l, j: (l, 0, j)),
        ],
        out_specs=pl.BlockSpec((1, N_COND_ROWS, tn), lambda l, j: (l, 0, j)),
        out_shape=jax.ShapeDtypeStruct((depth, N_COND_ROWS, n6), F32),
        compiler_params=pltpu.CompilerParams(
            dimension_semantics=("arbitrary", "arbitrary"), vmem_limit_bytes=VMEM_LIMIT),
        name="adaln",
    )(cond, w_ada, b_ada.reshape(depth, 1, n6))


def _qkv_kernel(*refs, n_x, n_prev, layer_i, n_prompt_tiles, scale):
    x_refs = refs[:n_x]
    mod_ref, g_ref, wq_ref, wt_ref = refs[n_x:n_x + 4]
    q_ref, ktl_ref, vtl_ref, kt_ref, vt_ref = refs[n_x + 4 + n_prev:]
    d = q_ref.shape[1]
    n_seq, seq = kt_ref.shape[0], kt_ref.shape[-1]
    is_prompt = pl.program_id(0) < n_prompt_tiles
    h = _rms_mod(_token_tile(x_refs, is_prompt), g_ref[...], mod_ref[0:1, :], mod_ref[1:2, :]).astype(BF16)
    q_ref[...] = (jnp.dot(h, wq_ref[...], preferred_element_type=F32) * scale).astype(BF16)

    @pl.when(is_prompt)
    def _():
        for s in range(n_seq):
            hs = h[s * seq:(s + 1) * seq, :]
            for t_ref, lo in ((kt_ref, 0), (vt_ref, d)):
                t = _nt_dot(wt_ref[lo:lo + d, :], hs)
                if n_prev:
                    t_ref[s] = t
                else:
                    for j in range(t_ref.shape[1]):
                        t_ref[s, j] = t if j == layer_i else jnp.zeros_like(t)

    @pl.when(jnp.logical_not(is_prompt))
    def _():
        ktl_ref[...] = _nt_dot(wt_ref[:d, :], h).astype(BF16)
        vtl_ref[...] = _nt_dot(wt_ref[d:, :], h).astype(BF16)


def _qkv(x_parts, mod_l, g, w_qkv, w_kv_t, prev_caches, *, layer_i, seq, n_prompt, n_tok, latent_seq, tm, scale):
    d = x_parts[0].shape[1]
    n_attn = w_qkv.shape[0]
    npt = n_prompt // tm
    tiles_per_seq = latent_seq // tm
    row = functools.partial(_cond_row, n_prompt_tiles=npt, tiles_per_latent_seq=tiles_per_seq)
    tok = pl.BlockSpec((tm, d), lambda i: (i, 0))
    latent_t = pl.BlockSpec((None, d, tm), lambda i: (jnp.maximum(i - npt, 0) // tiles_per_seq, 0,
                                                     jnp.maximum(i - npt, 0) % tiles_per_seq))
    if prev_caches:
        cache_blk = pl.BlockSpec((tm // seq, None, d, seq), lambda i: (jnp.minimum(i, npt - 1), layer_i, 0, 0))
    else:
        cache_blk = pl.BlockSpec((tm // seq, n_attn, d, seq), lambda i: (jnp.minimum(i, npt - 1), 0, 0, 0))
    cache_shape = jax.ShapeDtypeStruct((n_prompt // seq, n_attn, d, seq), F32)
    latent_shape = jax.ShapeDtypeStruct(((n_tok - n_prompt) // latent_seq, d, latent_seq), BF16)
    n_in = len(x_parts) + 4
    return pl.pallas_call(
        functools.partial(_qkv_kernel, n_x=len(x_parts), n_prev=len(prev_caches), layer_i=layer_i,
                          n_prompt_tiles=npt, scale=scale),
        grid=(n_tok // tm,),
        in_specs=_token_specs(x_parts, tm, npt) + [
            pl.BlockSpec((None, 6, d), lambda i: (row(i), 0, 0)),
            _const_spec((1, d)),
            _layer_spec((d, d), layer_i),
            _layer_spec((2 * d, d), layer_i),
        ] + [pl.BlockSpec(memory_space=pl.ANY)] * len(prev_caches),
        out_specs=[tok, latent_t, latent_t, cache_blk, cache_blk],
        out_shape=[jax.ShapeDtypeStruct((n_tok, d), BF16), latent_shape, latent_shape, cache_shape, cache_shape],
        input_output_aliases={n_in + j: 3 + j for j in range(len(prev_caches))},
        compiler_params=pltpu.CompilerParams(
            dimension_semantics=("arbitrary",), vmem_limit_bytes=VMEM_LIMIT),
        name="qkv",
    )(*x_parts, mod_l, g, w_qkv, w_kv_t, *prev_caches)


def _nt_dot(a, b):
    return lax.dot_general(a, b, (((1,), (1,)), ((), ())), preferred_element_type=F32)


def _pair_heads(qp, first):
    zero = jnp.zeros_like(qp)
    return jnp.where(first, qp, zero), jnp.where(first, zero, qp)


def _prompt_attn_kernel(q_ref, kt_ref, vt_ref, o_ref):
    seq, d = q_ref.shape
    half = LANE // 2
    first = lax.broadcasted_iota(jnp.int32, (seq, LANE), 1) < half
    ones = jnp.ones((2 * SUBLANE, seq), BF16)
    for p in range(d // LANE):
        sl = slice(p * LANE, (p + 1) * LANE)
        ktp = kt_ref[sl, :].astype(BF16)
        vtp = jnp.concatenate([vt_ref[sl, :].astype(BF16), ones], axis=0)
        outs = []
        for qm in _pair_heads(q_ref[:, sl], first):
            s = jnp.dot(qm, ktp, preferred_element_type=F32)
            e = jnp.exp(s - jnp.max(s, axis=-1, keepdims=True))
            ot = _nt_dot(vtp, e.astype(BF16))
            outs.append(ot[:LANE, :] / ot[LANE:LANE + 1, :])
        ot = jnp.concatenate([outs[0][:half, :], outs[1][half:, :]], axis=0)
        o_ref[:, sl] = ot.T.astype(BF16)


def _prompt_attn(q, cache_kt, cache_vt, *, layer_i, n_prompt, seq):
    d = q.shape[1]
    tok = pl.BlockSpec((seq, d), lambda b: (b, 0))
    feat = pl.BlockSpec((None, None, d, seq), lambda b: (b, layer_i, 0, 0))
    return pl.pallas_call(
        _prompt_attn_kernel,
        grid=(n_prompt // seq,),
        in_specs=[tok, feat, feat],
        out_specs=tok,
        out_shape=jax.ShapeDtypeStruct((n_prompt, d), BF16),
        compiler_params=pltpu.CompilerParams(
            dimension_semantics=("arbitrary",), vmem_limit_bytes=VMEM_LIMIT),
        name="prompt_attn",
    )(q, cache_kt, cache_vt)


def _row_start(r, rows):
    kr = min(WIN_ROWS, rows)
    return min(max(r - kr // 2, 0), rows - kr)


def _latent_block_plan(rows):
    kr = min(WIN_ROWS, rows)
    plan = []
    for qb in range(rows // ROWS_PER_BLOCK):
        starts = [_row_start(r, rows) for r in range(qb * ROWS_PER_BLOCK, (qb + 1) * ROWS_PER_BLOCK)]
        plan.append((min(starts) // ROWS_PER_BLOCK, (max(starts) + kr - 1) // ROWS_PER_BLOCK + 1))
    return plan


def _build_pair_bias(rpb_ref, tile_ref, bias_ref, head0, *, plan, rows):
    kr = min(WIN_ROWS, rows)
    shape = (GRID_W, LANE)
    qc = lax.broadcasted_iota(jnp.int32, shape, 0)
    lane = lax.broadcasted_iota(jnp.int32, shape, 1)
    kc = lane & (GRID_W - 1)
    delta = kc - qc + (WIN_COLS - 1)
    col_start = jnp.clip(qc - WIN_COLS // 2, 0, GRID_W - WIN_COLS)
    in_window = (kc >= col_start) & (kc < col_start + WIN_COLS)
    left = lane < GRID_W
    neg = jnp.full(shape, -jnp.inf, F32)
    for hh in range(2):
        base = (head0 + hh) * (N_DROW * N_DCOL)
        for dr in range(N_DROW):
            w = neg
            for dc in range(N_DCOL):
                w = jnp.where(delta == dc, rpb_ref[base + dr * N_DCOL + dc], w)
            tile_ref[dr] = jnp.where(in_window, w, neg)
        t = 0
        for qb, (lo, hi) in enumerate(plan):
            for kb in range(lo, hi):
                for a in range(ROWS_PER_BLOCK):
                    r = qb * ROWS_PER_BLOCK + a
                    start = _row_start(r, rows)
                    for c in range(ATT_BLOCK // LANE):
                        halves = []
                        for jj in range(LANE // GRID_W):
                            k_row = kb * ROWS_PER_BLOCK + c * (LANE // GRID_W) + jj
                            ok = start <= k_row < start + kr
                            halves.append(tile_ref[k_row - r + WIN_ROWS - 1] if ok else None)
                        if halves[0] is None and halves[1] is None:
                            val = neg
                        else:
                            val = jnp.where(left, neg if halves[0] is None else halves[0],
                                            neg if halves[1] is None else halves[1])
                        bias_ref[hh, t, a * GRID_W:(a + 1) * GRID_W, c * LANE:(c + 1) * LANE] = val
                t += 1


def _latent_attn_kernel(rpb_ref, q_ref, kt_ref, vt_ref, kct_ref, vct_ref, wup_ref, wdn_ref, o_ref, wup_o_ref,
                        wdn_o_ref, tile_ref, bias_ref, *, plan, rows, n_batch, n_chunks):
    _cast_chunks(pl.program_id(0) * n_batch + pl.program_id(1), n_chunks, (wup_ref, wdn_ref),
                 (wup_o_ref, wdn_o_ref))

    @pl.when(pl.program_id(1) == 0)
    def _():
        _build_pair_bias(rpb_ref, tile_ref, bias_ref, 2 * pl.program_id(0), plan=plan, rows=rows)

    half = LANE // 2
    first = lax.broadcasted_iota(jnp.int32, (ATT_BLOCK, LANE), 1) < half
    kct = kct_ref[...].astype(BF16)
    vct = jnp.concatenate([vct_ref[...].astype(BF16), jnp.ones((2 * SUBLANE, kct.shape[1]), BF16)], axis=0)
    vt = jnp.concatenate([vt_ref[...], jnp.ones((2 * SUBLANE, vt_ref.shape[1]), BF16)], axis=0)
    bias_at = 0
    for qb, (lo, hi) in enumerate(plan):
        keys = slice(lo * ATT_BLOCK, hi * ATT_BLOCK)
        outs = []
        for hh, qm in enumerate(_pair_heads(q_ref[qb * ATT_BLOCK:(qb + 1) * ATT_BLOCK, :], first)):
            bias = jnp.concatenate([bias_ref[hh, bias_at + j] for j in range(hi - lo)], axis=1)
            s_loc = jnp.dot(qm, kt_ref[:, keys], preferred_element_type=F32) + bias
            s_ctx = jnp.dot(qm, kct, preferred_element_type=F32)
            m = jnp.maximum(jnp.max(s_loc, axis=-1, keepdims=True), jnp.max(s_ctx, axis=-1, keepdims=True))
            e_loc = jnp.exp(s_loc - m).astype(BF16)
            e_ctx = jnp.exp(s_ctx - m).astype(BF16)
            ot = _nt_dot(vt[:, keys], e_loc) + _nt_dot(vct, e_ctx)
            outs.append(ot[:LANE, :] / ot[LANE:LANE + 1, :])
        bias_at += hi - lo
        ot = jnp.concatenate([outs[0][:half, :], outs[1][half:, :]], axis=0)
        o_ref[qb * ATT_BLOCK:(qb + 1) * ATT_BLOCK, :] = ot.T.astype(BF16)


def _latent_attn(rpb_flat, q, kt, vt, cache_kt, cache_vt, mlp_weights, *, layer_i, layer, n_prompt, latent_seq):
    d = q.shape[1]
    n_batch, _, _, past = cache_kt.shape
    rows = latent_seq // GRID_W
    plan = _latent_block_plan(rows)
    n_bias = sum(hi - lo for lo, hi in plan)
    first_latent = n_prompt // latent_seq
    n_steps = (d // LANE) * n_batch
    tok_q = pl.BlockSpec((latent_seq, LANE), lambda p, b: (first_latent + b, p))
    feat = pl.BlockSpec((None, LANE, latent_seq), lambda p, b: (b, p, 0))
    ctx = pl.BlockSpec((None, None, LANE, past), lambda p, b: (b, layer_i, p, 0))
    cast_in, cast_out, cast_shapes = _cast_specs(mlp_weights, layer, n_steps, lambda p, b: p * n_batch + b)
    return pl.pallas_call(
        functools.partial(_latent_attn_kernel, plan=plan, rows=rows, n_batch=n_batch, n_chunks=n_steps),
        grid=(d // LANE, n_batch),
        in_specs=[pl.BlockSpec(memory_space=pltpu.SMEM), tok_q, feat, feat, ctx, ctx] + cast_in,
        out_specs=[pl.BlockSpec((latent_seq, LANE), lambda p, b: (b, p))] + cast_out,
        out_shape=[jax.ShapeDtypeStruct((n_batch * latent_seq, d), BF16)] + cast_shapes,
        scratch_shapes=[pltpu.VMEM((N_DROW, GRID_W, LANE), F32),
                        pltpu.VMEM((2, n_bias, ATT_BLOCK, ATT_BLOCK), F32)],
        compiler_params=pltpu.CompilerParams(
            dimension_semantics=("arbitrary", "arbitrary"), vmem_limit_bytes=VMEM_LIMIT),
        name="latent_attn",
    )(rpb_flat, q, kt, vt, cache_kt, cache_vt, *mlp_weights)


def _conv_kernel(x_ref, mod_ref, g_ref, w1_ref, wdw_ref, bdw_ref, lng_ref, lnb_ref, wup_ref, wdn_ref, a_ref,
                 wup_o_ref, wdn_o_ref, pad_ref, *, n_prompt_tiles, prompt_seq, rows_per_step, n_chunks):
    _cast_chunks(pl.program_id(0), n_chunks, (wup_ref, wdn_ref), (wup_o_ref, wdn_o_ref))
    tm, d = x_ref.shape
    h = _rms_mod(x_ref[...], g_ref[...], mod_ref[0:1, :], mod_ref[1:2, :])
    ag = jnp.dot(h.astype(BF16), w1_ref[...], preferred_element_type=F32)
    u = ag[:, :d] * jax.nn.sigmoid(ag[:, d:])
    first_tap = CONV_PAD - CONV_WIDTH // 2
    rs = rows_per_step

    def conv(seq):
        stride = seq + 2 * CONV_PAD
        for s in range(tm // seq):
            base = s * stride
            pad_ref[base:base + CONV_PAD, :] = jnp.zeros((CONV_PAD, d), F32)
            pad_ref[base + CONV_PAD:base + CONV_PAD + seq, :] = u[s * seq:(s + 1) * seq, :]
            pad_ref[base + CONV_PAD + seq:base + stride, :] = jnp.zeros((CONV_PAD, d), F32)
        for s in range(tm // seq):
            base = s * stride

            def step(ci, carry, base=base, s=s):
                r0 = pl.multiple_of(ci * rs, rs)
                parts = []
                for lc in range(d // LANE):
                    ls = slice(lc * LANE, (lc + 1) * LANE)
                    win = pad_ref[pl.ds(base + r0, rs + 2 * CONV_PAD), ls]
                    acc = jnp.broadcast_to(bdw_ref[:, ls], (rs, LANE))
                    for b in range(SUBLANE):
                        vb = None
                        for o in range(b, CONV_WIDTH + first_tap, SUBLANE):
                            t = o - first_tap
                            if t < 0:
                                continue
                            term = win[o - b:o - b + rs + SUBLANE, :] * wdw_ref[t:t + 1, ls]
                            vb = term if vb is None else vb + term
                        if b:
                            vb = pltpu.roll(vb, rs + SUBLANE - b, axis=0)
                        acc = acc + vb[:rs, :]
                    parts.append(acc)
                acc = jnp.concatenate(parts, axis=1)
                mu = jnp.mean(acc, axis=-1, keepdims=True)
                xc = acc - mu
                y = xc * lax.rsqrt(jnp.mean(xc * xc, axis=-1, keepdims=True) + LN_EPS)
                y = y * lng_ref[...] + lnb_ref[...]
                a_ref[pl.ds(s * seq + r0, rs), :] = (y * jax.nn.sigmoid(y)).astype(BF16)
                return carry

            lax.fori_loop(0, seq // rs, step, 0)

    is_prompt = pl.program_id(0) < n_prompt_tiles
    pl.when(is_prompt)(lambda: conv(prompt_seq))
    pl.when(jnp.logical_not(is_prompt))(lambda: conv(tm))


def _conv_front(x, mod_l, g, w_pw1, w_dw, b_dw, ln_g, ln_b, mlp_weights, *, layer_i, layer, n_prompt, prompt_seq,
                latent_seq):
    n_tok, d = x.shape
    tm = latent_seq
    npt = n_prompt // tm
    n_chunks = 8
    assert n_tok // tm >= n_chunks
    row = functools.partial(_cond_row, n_prompt_tiles=npt, tiles_per_latent_seq=1)
    tok = pl.BlockSpec((tm, d), lambda i: (i, 0))
    pad_rows = (tm // prompt_seq) * (prompt_seq + 2 * CONV_PAD)
    cast_in, cast_out, cast_shapes = _cast_specs(mlp_weights, layer, n_chunks, lambda i: i)
    return pl.pallas_call(
        functools.partial(_conv_kernel, n_prompt_tiles=npt, prompt_seq=prompt_seq, rows_per_step=128,
                          n_chunks=n_chunks),
        grid=(n_tok // tm,),
        in_specs=[
            tok,
            pl.BlockSpec((None, 6, d), lambda i: (row(i), 0, 0)),
            _const_spec((1, d)),
            _layer_spec((d, 2 * d), layer_i),
            _const_spec((CONV_WIDTH, d)),
            _const_spec((1, d)),
            _const_spec((1, d)),
            _const_spec((1, d)),
        ] + cast_in,
        out_specs=[tok] + cast_out,
        out_shape=[jax.ShapeDtypeStruct((n_tok, d), BF16)] + cast_shapes,
        scratch_shapes=[pltpu.VMEM((pad_rows, d), F32)],
        compiler_params=pltpu.CompilerParams(
            dimension_semantics=("arbitrary",), vmem_limit_bytes=VMEM_LIMIT),
        name="conv_front",
    )(x, mod_l, g, w_pw1, w_dw, b_dw, ln_g, ln_b, *mlp_weights)


def _post_kernel(*refs, n_x, n_a, n_out, n_prompt_tiles, ff_chunk, final):
    x_refs, a_refs = refs[:n_x], refs[n_x:n_x + n_a]
    mod_ref, g_ref, wp_ref, wup_ref, wdn_ref, fg_ref = refs[n_x + n_a:n_x + n_a + 6]
    o_refs = refs[n_x + n_a + 6:]
    is_prompt = pl.program_id(0) < n_prompt_tiles
    y = jnp.dot(_token_tile(a_refs, is_prompt), wp_ref[...], preferred_element_type=F32)
    x1 = _token_tile(x_refs, is_prompt) + mod_ref[2:3, :] * y
    h = _rms_mod(x1, g_ref[...], mod_ref[3:4, :], mod_ref[4:5, :]).astype(BF16)
    acc = jnp.zeros_like(x1)
    for c in range(wup_ref.shape[1] // ff_chunk):
        sl = slice(c * ff_chunk, (c + 1) * ff_chunk)
        u = jnp.maximum(jnp.dot(h, wup_ref[:, sl], preferred_element_type=F32), 0.0)
        acc = acc + jnp.dot((u * u).astype(BF16), wdn_ref[sl, :], preferred_element_type=F32)
    out = x1 + mod_ref[5:6, :] * acc
    if final:
        out = out * lax.rsqrt(jnp.mean(out * out, axis=-1, keepdims=True) + RMS_EPS) * fg_ref[...]
    if n_out == 1:
        o_refs[0][...] = out
    else:
        @pl.when(is_prompt)
        def _():
            o_refs[0][...] = out

        @pl.when(jnp.logical_not(is_prompt))
        def _():
            o_refs[1][...] = out


def _post(x_parts, a_parts, mod_l, g, w_proj, w_up, w_down, final_g, *, proj_i, n_prompt, n_tok, latent_seq, tm,
          final):
    d = x_parts[0].shape[1]
    d_ff = w_up.shape[1]
    npt = n_prompt // tm
    row = functools.partial(_cond_row, n_prompt_tiles=npt, tiles_per_latent_seq=latent_seq // tm)
    if final:
        out_shape = [jax.ShapeDtypeStruct((n_prompt, d), F32), jax.ShapeDtypeStruct((n_tok - n_prompt, d), F32)]
    else:
        out_shape = [jax.ShapeDtypeStruct((n_tok, d), F32)]
    return pl.pallas_call(
        functools.partial(_post_kernel, n_x=len(x_parts), n_a=len(a_parts), n_out=len(out_shape),
                          n_prompt_tiles=npt, ff_chunk=1024, final=final),
        grid=(n_tok // tm,),
        in_specs=_token_specs(x_parts, tm, npt) + _token_specs(a_parts, tm, npt) + [
            pl.BlockSpec((None, 6, d), lambda i: (row(i), 0, 0)),
            _const_spec((1, d)),
            _layer_spec((d, d), proj_i),
            _const_spec((d, d_ff)),
            _const_spec((d_ff, d)),
            _const_spec((1, d)),
        ],
        out_specs=_token_specs(out_shape, tm, npt),
        out_shape=out_shape,
        compiler_params=pltpu.CompilerParams(
            dimension_semantics=("arbitrary",), vmem_limit_bytes=VMEM_LIMIT),
        name="post_mlp",
    )(*x_parts, *a_parts, mod_l, g, w_proj, w_up, w_down, final_g)


def kernel(x_prompt, x_sample, cache_k, cache_v, c, c_ctx, norm_g, w_ada, b_ada, w_qkv, w_o, rpb,
           w_pw1, w_dw, b_dw, conv_ln_g, conv_ln_b, w_pw2, w_up, w_down, final_g):
    batch, seq, d = x_prompt.shape
    dec_batch, dec_seq, _ = x_sample.shape
    depth = w_ada.shape[0]
    n_attn = w_qkv.shape[0]
    past = cache_k.shape[2]
    n_prompt = batch * seq
    n_tok = n_prompt + dec_batch * dec_seq
    head_dim = d // N_HEADS
    assert d == N_HEADS * head_dim and 2 * head_dim == LANE and LANE == 2 * GRID_W
    assert seq == ATT_BLOCK and past == ATT_BLOCK and dec_seq % ATT_BLOCK == 0
    assert rpb.shape[1:] == (N_HEADS, N_DROW, N_DCOL)
    assert 1 + dec_batch <= N_COND_ROWS
    tm = 512
    dims = dict(n_prompt=n_prompt, n_tok=n_tok, latent_seq=dec_seq)

    x_parts = [x_prompt.reshape(n_prompt, d), x_sample.reshape(dec_batch * dec_seq, d)]
    cond = jnp.concatenate([c_ctx[None, :], c, jnp.zeros((N_COND_ROWS - 1 - dec_batch, d), F32)], axis=0)
    mod = _adaln(cond, w_ada, b_ada).reshape(depth, N_COND_ROWS, 6, d)
    ckt = jnp.transpose(cache_k, (0, 1, 3, 4, 2)).reshape(dec_batch, n_attn, d, past)
    cvt = jnp.transpose(cache_v, (0, 1, 3, 4, 2)).reshape(dec_batch, n_attn, d, past)
    final_g2 = final_g.reshape(1, d)
    w_qkv_b, w_o_b, w_pw1_b, w_pw2_b = [w.astype(BF16) for w in (w_qkv, w_o, w_pw1, w_pw2)]
    w_kv_t = jnp.transpose(w_qkv_b[:, :, d:], (0, 2, 1))
    mlp_weights = (w_up, w_down)

    new_caches = ()
    for l in range(depth):
        i = l // 2
        mod_l = mod[l]
        g1 = norm_g[l, 0].reshape(1, d)
        g2 = norm_g[l, 1].reshape(1, d)
        if l % 2 == 0:
            q, kt, vt, *new_caches = _qkv(x_parts, mod_l, g1, w_qkv_b, w_kv_t, new_caches, layer_i=i, seq=seq,
                                          tm=tm, scale=head_dim ** -0.5, **dims)
            o_p = _prompt_attn(q, *new_caches, layer_i=i, n_prompt=n_prompt, seq=seq)
            o_s, w_up_l, w_down_l = _latent_attn(rpb[i].reshape(-1), q, kt, vt, ckt, cvt, mlp_weights, layer_i=i,
                                                 layer=l, n_prompt=n_prompt, latent_seq=dec_seq)
            a_parts = [o_p, o_s]
            w_proj = w_o_b
        else:
            assert len(x_parts) == 1
            a, w_up_l, w_down_l = _conv_front(x_parts[0], mod_l, g1, w_pw1_b, w_dw[i], b_dw[i].reshape(1, d),
                                              conv_ln_g[i].reshape(1, d), conv_ln_b[i].reshape(1, d), mlp_weights,
                                              layer_i=i, layer=l, n_prompt=n_prompt, prompt_seq=seq,
                                              latent_seq=dec_seq)
            a_parts = [a]
            w_proj = w_pw2_b
        x_parts = _post(x_parts, a_parts, mod_l, g2, w_proj, w_up_l, w_down_l, final_g2, proj_i=i, tm=tm,
                        final=(l == depth - 1), **dims)

    y_prompt, y_sample = x_parts
    new_k, new_v = [jnp.transpose(t.reshape(batch, n_attn, N_HEADS, head_dim, seq), (0, 1, 4, 2, 3))
                    for t in new_caches]
    return (y_prompt.reshape(batch, seq, d), y_sample.reshape(dec_batch, dec_seq, d), new_k, new_v)
```

```python
import functools

import numpy as np
import jax
import jax.numpy as jnp
from jax import lax
from jax.experimental import pallas as pl
from jax.experimental.pallas import tpu as pltpu

F32 = jnp.float32
BF16 = jnp.bfloat16

RMS_EPS = 1e-6
LN_EPS = 1e-5
N_HEADS = 16
GRID_W = 64
WIN_ROWS = 8
WIN_COLS = 16
N_DROW = 2 * WIN_ROWS - 1
N_DCOL = 2 * WIN_COLS - 1
CONV_WIDTH = 31
CONV_PAD = 16
N_COND_ROWS = 8

LANE = 128
SUBLANE = 8
ATT_BLOCK = 256
ROWS_PER_BLOCK = ATT_BLOCK // GRID_W
VMEM_LIMIT = 56 * 1024 * 1024


def _rms_mod(xf, g, shift, scale):
    y = xf * lax.rsqrt(jnp.mean(xf * xf, axis=-1, keepdims=True) + RMS_EPS)
    return (y * g) * (1.0 + scale) + shift


def _cond_row(i, n_prompt_tiles, tiles_per_latent_seq):
    return jnp.where(i < n_prompt_tiles, 0, 1 + (i - n_prompt_tiles) // tiles_per_latent_seq)


def _const_spec(shape):
    nd = len(shape)
    return pl.BlockSpec(shape, lambda *_: (0,) * nd, pipeline_mode=pl.Buffered(1))


def _layer_spec(shape, layer):
    nd = len(shape)
    return pl.BlockSpec((None,) + tuple(shape), lambda *_: (layer,) + (0,) * nd, pipeline_mode=pl.Buffered(1))


def _token_specs(parts, tm, n_prompt_tiles):
    d = parts[0].shape[1]
    if len(parts) == 1:
        return [pl.BlockSpec((tm, d), lambda i: (i, 0))]
    return [pl.BlockSpec((tm, d), lambda i: (jnp.minimum(i, n_prompt_tiles - 1), 0)),
            pl.BlockSpec((tm, d), lambda i: (jnp.maximum(i - n_prompt_tiles, 0), 0))]


def _token_tile(refs, is_prompt):
    if len(refs) == 1:
        return refs[0][...]
    return jnp.where(is_prompt, refs[0][...], refs[1][...])


def _cast_specs(weights, layer, n_chunks, step_of):
    in_specs, out_specs, out_shapes = [], [], []
    for w in weights:
        _, r, c = w.shape
        chunk = lambda *idx: jnp.minimum(step_of(*idx), n_chunks - 1)
        in_specs.append(pl.BlockSpec((None, r // n_chunks, c), lambda *idx, chunk=chunk: (layer, chunk(*idx), 0)))
        out_specs.append(pl.BlockSpec((r // n_chunks, c), lambda *idx, chunk=chunk: (chunk(*idx), 0)))
        out_shapes.append(jax.ShapeDtypeStruct((r, c), BF16))
    return in_specs, out_specs, out_shapes


def _cast_chunks(step, n_chunks, w_refs, o_refs):
    @pl.when(step < n_chunks)
    def _():
        for w_ref, o_ref in zip(w_refs, o_refs):
            o_ref[...] = w_ref[...].astype(BF16)


def _adaln_kernel(cond_ref, w_ref, b_ref, o_ref):
    s = cond_ref[...]
    s = s * jax.nn.sigmoid(s)
    o_ref[0] = jnp.dot(s.astype(BF16), w_ref[0].astype(BF16), preferred_element_type=F32) + b_ref[0]


def _adaln(cond, w_ada, b_ada):
    depth, d, n6 = w_ada.shape
    tn = n6 // 4
    return pl.pallas_call(
        _adaln_kernel,
        grid=(depth, n6 // tn),
        in_specs=[
            pl.BlockSpec((N_COND_ROWS, d), lambda l, j: (0, 0)),
            pl.BlockSpec((1, d, tn), lambda l, j: (l, 0, j)),
            pl.BlockSpec((1, 1, tn), lambda l, j: (l, 0, j)),
        ],
        out_specs=pl.BlockSpec((1, N_COND_ROWS, tn), lambda l, j: (l, 0, j)),
        out_shape=jax.ShapeDtypeStruct((depth, N_COND_ROWS, n6), F32),
        compiler_params=pltpu.CompilerParams(
            dimension_semantics=("arbitrary", "arbitrary"), vmem_limit_bytes=VMEM_LIMIT),
        name="adaln",
    )(cond, w_ada, b_ada.reshape(depth, 1, n6))


def _qkv_kernel(*refs, n_x, n_prev, layer_i, n_prompt_tiles, scale):
    x_refs = refs[:n_x]
    mod_ref, g_ref, w_ref = refs[n_x:n_x + 3]
    q_ref, ktl_ref, vtl_ref, kt_ref, vt_ref, wt_ref = refs[n_x + 3 + n_prev:]
    d = q_ref.shape[1]
    n_seq, seq = kt_ref.shape[0], kt_ref.shape[-1]

    @pl.when(pl.program_id(0) == 0)
    def _():
        for r in range(0, 2 * d, ATT_BLOCK):
            for c in range(0, d, ATT_BLOCK):
                wt_ref[r:r + ATT_BLOCK, c:c + ATT_BLOCK] = w_ref[c:c + ATT_BLOCK, d + r:d + r + ATT_BLOCK].T

    is_prompt = pl.program_id(0) < n_prompt_tiles
    h = _rms_mod(_token_tile(x_refs, is_prompt), g_ref[...], mod_ref[0:1, :], mod_ref[1:2, :]).astype(BF16)
    q_ref[...] = (jnp.dot(h, w_ref[:, :d], preferred_element_type=F32) * scale).astype(BF16)

    @pl.when(is_prompt)
    def _():
        for s in range(n_seq):
            hs = h[s * seq:(s + 1) * seq, :]
            for t_ref, lo in ((kt_ref, 0), (vt_ref, d)):
                t = _nt_dot(wt_ref[lo:lo + d, :], hs)
                if n_prev:
                    t_ref[s] = t
                else:
                    for j in range(t_ref.shape[1]):
                        t_ref[s, j] = t if j == layer_i else jnp.zeros_like(t)

    @pl.when(jnp.logical_not(is_prompt))
    def _():
        ktl_ref[...] = _nt_dot(wt_ref[:d, :], h).astype(BF16)
        vtl_ref[...] = _nt_dot(wt_ref[d:, :], h).astype(BF16)


def _qkv(x_parts, mod_l, g, w_qkv, prev_caches, *, layer_i, seq, n_prompt, n_tok, latent_seq, tm, scale):
    d = x_parts[0].shape[1]
    n_attn = w_qkv.shape[0]
    npt = n_prompt // tm
    tiles_per_seq = latent_seq // tm
    row = functools.partial(_cond_row, n_prompt_tiles=npt, tiles_per_latent_seq=tiles_per_seq)
    tok = pl.BlockSpec((tm, d), lambda i: (i, 0))
    latent_t = pl.BlockSpec((None, d, tm), lambda i: (jnp.maximum(i - npt, 0) // tiles_per_seq, 0,
                                                     jnp.maximum(i - npt, 0) % tiles_per_seq))
    if prev_caches:
        cache_blk = pl.BlockSpec((tm // seq, None, d, seq), lambda i: (jnp.minimum(i, npt - 1), layer_i, 0, 0))
    else:
        cache_blk = pl.BlockSpec((tm // seq, n_attn, d, seq), lambda i: (jnp.minimum(i, npt - 1), 0, 0, 0))
    cache_shape = jax.ShapeDtypeStruct((n_prompt // seq, n_attn, d, seq), F32)
    latent_shape = jax.ShapeDtypeStruct(((n_tok - n_prompt) // latent_seq, d, latent_seq), BF16)
    n_in = len(x_parts) + 3
    return pl.pallas_call(
        functools.partial(_qkv_kernel, n_x=len(x_parts), n_prev=len(prev_caches), layer_i=layer_i,
                          n_prompt_tiles=npt, scale=scale),
        grid=(n_tok // tm,),
        in_specs=_token_specs(x_parts, tm, npt) + [
            pl.BlockSpec((None, 6, d), lambda i: (row(i), 0, 0)),
            _const_spec((1, d)),
            _layer_spec((d, 3 * d), layer_i),
        ] + [pl.BlockSpec(memory_space=pl.ANY)] * len(prev_caches),
        out_specs=[tok, latent_t, latent_t, cache_blk, cache_blk],
        out_shape=[jax.ShapeDtypeStruct((n_tok, d), BF16), latent_shape, latent_shape, cache_shape, cache_shape],
        input_output_aliases={n_in + j: 3 + j for j in range(len(prev_caches))},
        scratch_shapes=[pltpu.VMEM((2 * d, d), BF16)],
        compiler_params=pltpu.CompilerParams(
            dimension_semantics=("arbitrary",), vmem_limit_bytes=VMEM_LIMIT),
        name="qkv",
    )(*x_parts, mod_l, g, w_qkv, *prev_caches)


def _nt_dot(a, b):
    return lax.dot_general(a, b, (((1,), (1,)), ((), ())), preferred_element_type=F32)


def _pair_heads(qp, first):
    zero = jnp.zeros_like(qp)
    return jnp.where(first, qp, zero), jnp.where(first, zero, qp)


def _prompt_attn_kernel(q_ref, kt_ref, vt_ref, o_ref):
    seq, d = q_ref.shape
    half = LANE // 2
    first = lax.broadcasted_iota(jnp.int32, (seq, LANE), 1) < half
    ones = jnp.ones((2 * SUBLANE, seq), BF16)
    for p in range(d // LANE):
        sl = slice(p * LANE, (p + 1) * LANE)
        ktp = kt_ref[sl, :].astype(BF16)
        vtp = jnp.concatenate([vt_ref[sl, :].astype(BF16), ones], axis=0)
        outs = []
        for qm in _pair_heads(q_ref[:, sl], first):
            s = jnp.dot(qm, ktp, preferred_element_type=F32)
            e = jnp.exp(s - jnp.max(s, axis=-1, keepdims=True))
            ot = _nt_dot(vtp, e.astype(BF16))
            outs.append(ot[:LANE, :] / ot[LANE:LANE + 1, :])
        ot = jnp.concatenate([outs[0][:half, :], outs[1][half:, :]], axis=0)
        o_ref[:, sl] = ot.T.astype(BF16)


def _prompt_attn(q, cache_kt, cache_vt, *, layer_i, n_prompt, seq):
    d = q.shape[1]
    tok = pl.BlockSpec((seq, d), lambda b: (b, 0))
    feat = pl.BlockSpec((None, None, d, seq), lambda b: (b, layer_i, 0, 0))
    return pl.pallas_call(
        _prompt_attn_kernel,
        grid=(n_prompt // seq,),
        in_specs=[tok, feat, feat],
        out_specs=tok,
        out_shape=jax.ShapeDtypeStruct((n_prompt, d), BF16),
        compiler_params=pltpu.CompilerParams(
            dimension_semantics=("arbitrary",), vmem_limit_bytes=VMEM_LIMIT),
        name="prompt_attn",
    )(q, cache_kt, cache_vt)


def _row_start(r, rows):
    kr = min(WIN_ROWS, rows)
    return min(max(r - kr // 2, 0), rows - kr)


def _latent_block_plan(rows):
    kr = min(WIN_ROWS, rows)
    plan = []
    for qb in range(rows // ROWS_PER_BLOCK):
        starts = [_row_start(r, rows) for r in range(qb * ROWS_PER_BLOCK, (qb + 1) * ROWS_PER_BLOCK)]
        plan.append((min(starts) // ROWS_PER_BLOCK, (max(starts) + kr - 1) // ROWS_PER_BLOCK + 1))
    return plan


def _build_pair_bias(rpb_ref, tile_ref, bias_ref, head0, *, plan, rows):
    kr = min(WIN_ROWS, rows)
    shape = (GRID_W, LANE)
    qc = lax.broadcasted_iota(jnp.int32, shape, 0)
    lane = lax.broadcasted_iota(jnp.int32, shape, 1)
    kc = lane & (GRID_W - 1)
    delta = kc - qc + (WIN_COLS - 1)
    col_start = jnp.clip(qc - WIN_COLS // 2, 0, GRID_W - WIN_COLS)
    in_window = (kc >= col_start) & (kc < col_start + WIN_COLS)
    left = lane < GRID_W
    neg = jnp.full(shape, -jnp.inf, F32)
    for hh in range(2):
        base = (head0 + hh) * (N_DROW * N_DCOL)
        for dr in range(N_DROW):
            w = neg
            for dc in range(N_DCOL):
                w = jnp.where(delta == dc, rpb_ref[base + dr * N_DCOL + dc], w)
            tile_ref[dr] = jnp.where(in_window, w, neg)
        t = 0
        for qb, (lo, hi) in enumerate(plan):
            for kb in range(lo, hi):
                for a in range(ROWS_PER_BLOCK):
                    r = qb * ROWS_PER_BLOCK + a
                    start = _row_start(r, rows)
                    for c in range(ATT_BLOCK // LANE):
                        halves = []
                        for jj in range(LANE // GRID_W):
                            k_row = kb * ROWS_PER_BLOCK + c * (LANE // GRID_W) + jj
                            ok = start <= k_row < start + kr
                            halves.append(tile_ref[k_row - r + WIN_ROWS - 1] if ok else None)
                        if halves[0] is None and halves[1] is None:
                            val = neg
                        else:
                            val = jnp.where(left, neg if halves[0] is None else halves[0],
                                            neg if halves[1] is None else halves[1])
                        bias_ref[hh, t, a * GRID_W:(a + 1) * GRID_W, c * LANE:(c + 1) * LANE] = val
                t += 1


def _latent_attn_kernel(rpb_ref, q_ref, kt_ref, vt_ref, kct_ref, vct_ref, wup_ref, wdn_ref, o_ref, wup_o_ref,
                        wdn_o_ref, tile_ref, bias_ref, *, plan, rows, n_batch, n_chunks):
    _cast_chunks(pl.program_id(0) * n_batch + pl.program_id(1), n_chunks, (wup_ref, wdn_ref),
                 (wup_o_ref, wdn_o_ref))

    @pl.when(pl.program_id(1) == 0)
    def _():
        _build_pair_bias(rpb_ref, tile_ref, bias_ref, 2 * pl.program_id(0), plan=plan, rows=rows)

    half = LANE // 2
    first = lax.broadcasted_iota(jnp.int32, (ATT_BLOCK, LANE), 1) < half
    kct = kct_ref[...].astype(BF16)
    vct = jnp.concatenate([vct_ref[...].astype(BF16), jnp.ones((2 * SUBLANE, kct.shape[1]), BF16)], axis=0)
    vt = jnp.concatenate([vt_ref[...], jnp.ones((2 * SUBLANE, vt_ref.shape[1]), BF16)], axis=0)
    bias_at = 0
    for qb, (lo, hi) in enumerate(plan):
        keys = slice(lo * ATT_BLOCK, hi * ATT_BLOCK)
        outs = []
        for hh, qm in enumerate(_pair_heads(q_ref[qb * ATT_BLOCK:(qb + 1) * ATT_BLOCK, :], first)):
            bias = jnp.concatenate([bias_ref[hh, bias_at + j] for j in range(hi - lo)], axis=1)
            s_loc = jnp.dot(qm, kt_ref[:, keys], preferred_element_type=F32) + bias
            s_ctx = jnp.dot(qm, kct, preferred_element_type=F32)
            m = jnp.maximum(jnp.max(s_loc, axis=-1, keepdims=True), jnp.max(s_ctx, axis=-1, keepdims=True))
            e_loc = jnp.exp(s_loc - m).astype(BF16)
            e_ctx = jnp.exp(s_ctx - m).astype(BF16)
            ot = _nt_dot(vt[:, keys], e_loc) + _nt_dot(vct, e_ctx)
            outs.append(ot[:LANE, :] / ot[LANE:LANE + 1, :])
        bias_at += hi - lo
        ot = jnp.concatenate([outs[0][:half, :], outs[1][half:, :]], axis=0)
        o_ref[qb * ATT_BLOCK:(qb + 1) * ATT_BLOCK, :] = ot.T.astype(BF16)


def _latent_attn(rpb_flat, q, kt, vt, cache_kt, cache_vt, mlp_weights, *, layer_i, layer, n_prompt, latent_seq):
    d = q.shape[1]
    n_batch, _, _, past = cache_kt.shape
    rows = latent_seq // GRID_W
    plan = _latent_block_plan(rows)
    n_bias = sum(hi - lo for lo, hi in plan)
    first_latent = n_prompt // latent_seq
    n_steps = (d // LANE) * n_batch
    tok_q = pl.BlockSpec((latent_seq, LANE), lambda p, b: (first_latent + b, p))
    feat = pl.BlockSpec((None, LANE, latent_seq), lambda p, b: (b, p, 0))
    ctx = pl.BlockSpec((None, None, LANE, past), lambda p, b: (b, layer_i, p, 0))
    cast_in, cast_out, cast_shapes = _cast_specs(mlp_weights, layer, n_steps, lambda p, b: p * n_batch + b)
    return pl.pallas_call(
        functools.partial(_latent_attn_kernel, plan=plan, rows=rows, n_batch=n_batch, n_chunks=n_steps),
        grid=(d // LANE, n_batch),
        in_specs=[pl.BlockSpec(memory_space=pltpu.SMEM), tok_q, feat, feat, ctx, ctx] + cast_in,
        out_specs=[pl.BlockSpec((latent_seq, LANE), lambda p, b: (b, p))] + cast_out,
        out_shape=[jax.ShapeDtypeStruct((n_batch * latent_seq, d), BF16)] + cast_shapes,
        scratch_shapes=[pltpu.VMEM((N_DROW, GRID_W, LANE), F32),
                        pltpu.VMEM((2, n_bias, ATT_BLOCK, ATT_BLOCK), F32)],
        compiler_params=pltpu.CompilerParams(
            dimension_semantics=("arbitrary", "arbitrary"), vmem_limit_bytes=VMEM_LIMIT),
        name="latent_attn",
    )(rpb_flat, q, kt, vt, cache_kt, cache_vt, *mlp_weights)


def _conv_kernel(x_ref, mod_ref, g_ref, w1_ref, wdw_ref, bdw_ref, lng_ref, lnb_ref, wup_ref, wdn_ref, a_ref,
                 wup_o_ref, wdn_o_ref, pad_ref, *, n_prompt_tiles, prompt_seq, rows_per_step, n_chunks):
    _cast_chunks(pl.program_id(0), n_chunks, (wup_ref, wdn_ref), (wup_o_ref, wdn_o_ref))
    tm, d = x_ref.shape
    h = _rms_mod(x_ref[...], g_ref[...], mod_ref[0:1, :], mod_ref[1:2, :])
    ag = jnp.dot(h.astype(BF16), w1_ref[...], preferred_element_type=F32)
    u = ag[:, :d] * jax.nn.sigmoid(ag[:, d:])
    first_tap = CONV_PAD - CONV_WIDTH // 2
    rs = rows_per_step

    def conv(seq):
        stride = seq + 2 * CONV_PAD
        for s in range(tm // seq):
            base = s * stride
            pad_ref[base:base + CONV_PAD, :] = jnp.zeros((CONV_PAD, d), F32)
            pad_ref[base + CONV_PAD:base + CONV_PAD + seq, :] = u[s * seq:(s + 1) * seq, :]
            pad_ref[base + CONV_PAD + seq:base + stride, :] = jnp.zeros((CONV_PAD, d), F32)
        for s in range(tm // seq):
            base = s * stride

            def step(ci, carry, base=base, s=s):
                r0 = pl.multiple_of(ci * rs, rs)
                parts = []
                for lc in range(d // LANE):
                    ls = slice(lc * LANE, (lc + 1) * LANE)
                    win = pad_ref[pl.ds(base + r0, rs + 2 * CONV_PAD), ls]
                    acc = jnp.broadcast_to(bdw_ref[:, ls], (rs, LANE))
                    for b in range(SUBLANE):
                        vb = None
                        for o in range(b, CONV_WIDTH + first_tap, SUBLANE):
                            t = o - first_tap
                            if t < 0:
                                continue
                            term = win[o - b:o - b + rs + SUBLANE, :] * wdw_ref[t:t + 1, ls]
                            vb = term if vb is None else vb + term
                        if b:
                            vb = pltpu.roll(vb, rs + SUBLANE - b, axis=0)
                        acc = acc + vb[:rs, :]
                    parts.append(acc)
                acc = jnp.concatenate(parts, axis=1)
                mu = jnp.mean(acc, axis=-1, keepdims=True)
                xc = acc - mu
                y = xc * lax.rsqrt(jnp.mean(xc * xc, axis=-1, keepdims=True) + LN_EPS)
                y = y * lng_ref[...] + lnb_ref[...]
                a_ref[pl.ds(s * seq + r0, rs), :] = (y * jax.nn.sigmoid(y)).astype(BF16)
                return carry

            lax.fori_loop(0, seq // rs, step, 0)

    is_prompt = pl.program_id(0) < n_prompt_tiles
    pl.when(is_prompt)(lambda: conv(prompt_seq))
    pl.when(jnp.logical_not(is_prompt))(lambda: conv(tm))


def _conv_front(x, mod_l, g, w_pw1, w_dw, b_dw, ln_g, ln_b, mlp_weights, *, layer_i, layer, n_prompt, prompt_seq,
                latent_seq):
    n_tok, d = x.shape
    tm = latent_seq
    npt = n_prompt // tm
    n_chunks = 8
    assert n_tok // tm >= n_chunks
    row = functools.partial(_cond_row, n_prompt_tiles=npt, tiles_per_latent_seq=1)
    tok = pl.BlockSpec((tm, d), lambda i: (i, 0))
    pad_rows = (tm // prompt_seq) * (prompt_seq + 2 * CONV_PAD)
    cast_in, cast_out, cast_shapes = _cast_specs(mlp_weights, layer, n_chunks, lambda i: i)
    return pl.pallas_call(
        functools.partial(_conv_kernel, n_prompt_tiles=npt, prompt_seq=prompt_seq, rows_per_step=128,
                          n_chunks=n_chunks),
        grid=(n_tok // tm,),
        in_specs=[
            tok,
            pl.BlockSpec((None, 6, d), lambda i: (row(i), 0, 0)),
            _const_spec((1, d)),
            _layer_spec((d, 2 * d), layer_i),
            _const_spec((CONV_WIDTH, d)),
            _const_spec((1, d)),
            _const_spec((1, d)),
            _const_spec((1, d)),
        ] + cast_in,
        out_specs=[tok] + cast_out,
        out_shape=[jax.ShapeDtypeStruct((n_tok, d), BF16)] + cast_shapes,
        scratch_shapes=[pltpu.VMEM((pad_rows, d), F32)],
        compiler_params=pltpu.CompilerParams(
            dimension_semantics=("arbitrary",), vmem_limit_bytes=VMEM_LIMIT),
        name="conv_front",
    )(x, mod_l, g, w_pw1, w_dw, b_dw, ln_g, ln_b, *mlp_weights)


def _post_kernel(*refs, n_x, n_a, n_out, n_prompt_tiles, ff_chunk, final):
    x_refs, a_refs = refs[:n_x], refs[n_x:n_x + n_a]
    mod_ref, g_ref, wp_ref, wup_ref, wdn_ref, fg_ref = refs[n_x + n_a:n_x + n_a + 6]
    o_refs = refs[n_x + n_a + 6:]
    is_prompt = pl.program_id(0) < n_prompt_tiles
    y = jnp.dot(_token_tile(a_refs, is_prompt), wp_ref[...], preferred_element_type=F32)
    x1 = _token_tile(x_refs, is_prompt) + mod_ref[2:3, :] * y
    h = _rms_mod(x1, g_ref[...], mod_ref[3:4, :], mod_ref[4:5, :]).astype(BF16)
    acc = jnp.zeros_like(x1)
    for c in range(wup_ref.shape[1] // ff_chunk):
        sl = slice(c * ff_chunk, (c + 1) * ff_chunk)
        u = jnp.maximum(jnp.dot(h, wup_ref[:, sl], preferred_element_type=F32), 0.0)
        acc = acc + jnp.dot((u * u).astype(BF16), wdn_ref[sl, :], preferred_element_type=F32)
    out = x1 + mod_ref[5:6, :] * acc
    if final:
        out = out * lax.rsqrt(jnp.mean(out * out, axis=-1, keepdims=True) + RMS_EPS) * fg_ref[...]
    if n_out == 1:
        o_refs[0][...] = out
    else:
        @pl.when(is_prompt)
        def _():
            o_refs[0][...] = out

        @pl.when(jnp.logical_not(is_prompt))
        def _():
            o_refs[1][...] = out


def _post(x_parts, a_parts, mod_l, g, w_proj, w_up, w_down, final_g, *, proj_i, n_prompt, n_tok, latent_seq, tm,
          final):
    d = x_parts[0].shape[1]
    d_ff = w_up.shape[1]
    npt = n_prompt // tm
    row = functools.partial(_cond_row, n_prompt_tiles=npt, tiles_per_latent_seq=latent_seq // tm)
    if final:
        out_shape = [jax.ShapeDtypeStruct((n_prompt, d), F32), jax.ShapeDtypeStruct((n_tok - n_prompt, d), F32)]
    else:
        out_shape = [jax.ShapeDtypeStruct((n_tok, d), F32)]
    return pl.pallas_call(
        functools.partial(_post_kernel, n_x=len(x_parts), n_a=len(a_parts), n_out=len(out_shape),
                          n_prompt_tiles=npt, ff_chunk=1024, final=final),
        grid=(n_tok // tm,),
        in_specs=_token_specs(x_parts, tm, npt) + _token_specs(a_parts, tm, npt) + [
            pl.BlockSpec((None, 6, d), lambda i: (row(i), 0, 0)),
            _const_spec((1, d)),
            _layer_spec((d, d), proj_i),
            _const_spec((d, d_ff)),
            _const_spec((d_ff, d)),
            _const_spec((1, d)),
        ],
        out_specs=_token_specs(out_shape, tm, npt),
        out_shape=out_shape,
        compiler_params=pltpu.CompilerParams(
            dimension_semantics=("arbitrary",), vmem_limit_bytes=VMEM_LIMIT),
        name="post_mlp",
    )(*x_parts, *a_parts, mod_l, g, w_proj, w_up, w_down, final_g)


def kernel(x_prompt, x_sample, cache_k, cache_v, c, c_ctx, norm_g, w_ada, b_ada, w_qkv, w_o, rpb,
           w_pw1, w_dw, b_dw, conv_ln_g, conv_ln_b, w_pw2, w_up, w_down, final_g):
    batch, seq, d = x_prompt.shape
    dec_batch, dec_seq, _ = x_sample.shape
    depth = w_ada.shape[0]
    n_attn = w_qkv.shape[0]
    past = cache_k.shape[2]
    n_prompt = batch * seq
    n_tok = n_prompt + dec_batch * dec_seq
    head_dim = d // N_HEADS
    assert d == N_HEADS * head_dim and 2 * head_dim == LANE and LANE == 2 * GRID_W
    assert seq == ATT_BLOCK and past == ATT_BLOCK and dec_seq % ATT_BLOCK == 0
    assert rpb.shape[1:] == (N_HEADS, N_DROW, N_DCOL)
    assert 1 + dec_batch <= N_COND_ROWS
    tm = 512
    dims = dict(n_prompt=n_prompt, n_tok=n_tok, latent_seq=dec_seq)

    x_parts = [x_prompt.reshape(n_prompt, d), x_sample.reshape(dec_batch * dec_seq, d)]
    cond = jnp.concatenate([c_ctx[None, :], c, jnp.zeros((N_COND_ROWS - 1 - dec_batch, d), F32)], axis=0)
    mod = _adaln(cond, w_ada, b_ada).reshape(depth, N_COND_ROWS, 6, d)
    ckt = jnp.transpose(cache_k, (0, 1, 3, 4, 2)).reshape(dec_batch, n_attn, d, past)
    cvt = jnp.transpose(cache_v, (0, 1, 3, 4, 2)).reshape(dec_batch, n_attn, d, past)
    final_g2 = final_g.reshape(1, d)
    w_qkv_b, w_o_b, w_pw1_b, w_pw2_b = [w.astype(BF16) for w in (w_qkv, w_o, w_pw1, w_pw2)]
    mlp_weights = (w_up, w_down)

    new_caches = ()
    for l in range(depth):
        i = l // 2
        mod_l = mod[l]
        g1 = norm_g[l, 0].reshape(1, d)
        g2 = norm_g[l, 1].reshape(1, d)
        if l % 2 == 0:
            q, kt, vt, *new_caches = _qkv(x_parts, mod_l, g1, w_qkv_b, new_caches, layer_i=i, seq=seq, tm=tm,
                                          scale=head_dim ** -0.5, **dims)
            o_p = _prompt_attn(q, *new_caches, layer_i=i, n_prompt=n_prompt, seq=seq)
            o_s, w_up_l, w_down_l = _latent_attn(rpb[i].reshape(-1), q, kt, vt, ckt, cvt, mlp_weights, layer_i=i,
                                                 layer=l, n_prompt=n_prompt, latent_seq=dec_seq)
            a_parts = [o_p, o_s]
            w_proj = w_o_b
        else:
            assert len(x_parts) == 1
            a, w_up_l, w_down_l = _conv_front(x_parts[0], mod_l, g1, w_pw1_b, w_dw[i], b_dw[i].reshape(1, d),
                                              conv_ln_g[i].reshape(1, d), conv_ln_b[i].reshape(1, d), mlp_weights,
                                              layer_i=i, layer=l, n_prompt=n_prompt, prompt_seq=seq,
                                              latent_seq=dec_seq)
            a_parts = [a]
            w_proj = w_pw2_b
        x_parts = _post(x_parts, a_parts, mod_l, g2, w_proj, w_up_l, w_down_l, final_g2, proj_i=i, tm=tm,
                        final=(l == depth - 1), **dims)

    y_prompt, y_sample = x_parts
    new_k, new_v = [jnp.transpose(t.reshape(batch, n_attn, N_HEADS, head_dim, seq), (0, 1, 4, 2, 3))
                    for t in new_caches]
    return (y_prompt.reshape(batch, seq, d), y_sample.reshape(dec_batch, dec_seq, d), new_k, new_v)
```

```python
import functools

import numpy as np
import jax
import jax.numpy as jnp
from jax import lax
from jax.experimental import pallas as pl
from jax.experimental.pallas import tpu as pltpu

F32 = jnp.float32
BF16 = jnp.bfloat16

RMS_EPS = 1e-6
LN_EPS = 1e-5
N_HEADS = 16
GRID_W = 64
WIN_ROWS = 8
WIN_COLS = 16
N_DROW = 2 * WIN_ROWS - 1
N_DCOL = 2 * WIN_COLS - 1
CONV_WIDTH = 31
CONV_PAD = 16
N_COND_ROWS = 8

LANE = 128
SUBLANE = 8
ATT_BLOCK = 256
ROWS_PER_BLOCK = ATT_BLOCK // GRID_W
VMEM_LIMIT = 56 * 1024 * 1024


def _rms_mod(xf, g, shift, scale):
    y = xf * lax.rsqrt(jnp.mean(xf * xf, axis=-1, keepdims=True) + RMS_EPS)
    return (y * g) * (1.0 + scale) + shift


def _cond_row(i, n_prompt_tiles, tiles_per_latent_seq):
    return jnp.where(i < n_prompt_tiles, 0, 1 + (i - n_prompt_tiles) // tiles_per_latent_seq)


def _const_spec(shape):
    nd = len(shape)
    return pl.BlockSpec(shape, lambda *_: (0,) * nd, pipeline_mode=pl.Buffered(1))


def _layer_spec(shape, layer):
    nd = len(shape)
    return pl.BlockSpec((None,) + tuple(shape), lambda *_: (layer,) + (0,) * nd, pipeline_mode=pl.Buffered(1))


def _token_specs(parts, tm, n_prompt_tiles):
    d = parts[0].shape[1]
    if len(parts) == 1:
        return [pl.BlockSpec((tm, d), lambda i: (i, 0))]
    return [pl.BlockSpec((tm, d), lambda i: (jnp.minimum(i, n_prompt_tiles - 1), 0)),
            pl.BlockSpec((tm, d), lambda i: (jnp.maximum(i - n_prompt_tiles, 0), 0))]


def _token_tile(refs, is_prompt):
    if len(refs) == 1:
        return refs[0][...]
    return jnp.where(is_prompt, refs[0][...], refs[1][...])


def _cast_specs(weights, layer, n_chunks, step_of):
    in_specs, out_specs, out_shapes = [], [], []
    for w in weights:
        _, r, c = w.shape
        chunk = lambda *idx: jnp.minimum(step_of(*idx), n_chunks - 1)
        in_specs.append(pl.BlockSpec((None, r // n_chunks, c), lambda *idx, chunk=chunk: (layer, chunk(*idx), 0)))
        out_specs.append(pl.BlockSpec((r // n_chunks, c), lambda *idx, chunk=chunk: (chunk(*idx), 0)))
        out_shapes.append(jax.ShapeDtypeStruct((r, c), BF16))
    return in_specs, out_specs, out_shapes


def _cast_chunks(step, n_chunks, w_refs, o_refs):
    @pl.when(step < n_chunks)
    def _():
        for w_ref, o_ref in zip(w_refs, o_refs):
            o_ref[...] = w_ref[...].astype(BF16)


def _adaln_kernel(cond_ref, w_ref, b_ref, o_ref):
    s = cond_ref[...]
    s = s * jax.nn.sigmoid(s)
    o_ref[0] = jnp.dot(s.astype(BF16), w_ref[0].astype(BF16), preferred_element_type=F32) + b_ref[0]


def _adaln(cond, w_ada, b_ada):
    depth, d, n6 = w_ada.shape
    tn = n6 // 4
    return pl.pallas_call(
        _adaln_kernel,
        grid=(depth, n6 // tn),
        in_specs=[
            pl.BlockSpec((N_COND_ROWS, d), lambda l, j: (0, 0)),
            pl.BlockSpec((1, d, tn), lambda l, j: (l, 0, j)),
            pl.BlockSpec((1, 1, tn), lambda l, j: (l, 0, j)),
        ],
        out_specs=pl.BlockSpec((1, N_COND_ROWS, tn), lambda l, j: (l, 0, j)),
        out_shape=jax.ShapeDtypeStruct((depth, N_COND_ROWS, n6), F32),
        compiler_params=pltpu.CompilerParams(
            dimension_semantics=("arbitrary", "arbitrary"), vmem_limit_bytes=VMEM_LIMIT),
        name="adaln",
    )(cond, w_ada, b_ada.reshape(depth, 1, n6))


def _qkv_kernel(*refs, n_x, n_prev, layer_i, n_prompt_tiles, scale):
    x_refs = refs[:n_x]
    mod_ref, g_ref, w_ref = refs[n_x:n_x + 3]
    q_ref, ktl_ref, vtl_ref, kt_ref, vt_ref, wt_ref = refs[n_x + 3 + n_prev:]
    d = q_ref.shape[1]
    n_seq, seq = kt_ref.shape[0], kt_ref.shape[-1]

    @pl.when(pl.program_id(0) == 0)
    def _():
        for r in range(0, 2 * d, ATT_BLOCK):
            for c in range(0, d, ATT_BLOCK):
                wt_ref[r:r + ATT_BLOCK, c:c + ATT_BLOCK] = w_ref[c:c + ATT_BLOCK, d + r:d + r + ATT_BLOCK].T

    is_prompt = pl.program_id(0) < n_prompt_tiles
    h = _rms_mod(_token_tile(x_refs, is_prompt), g_ref[...], mod_ref[0:1, :], mod_ref[1:2, :]).astype(BF16)
    q_ref[...] = (jnp.dot(h, w_ref[:, :d], preferred_element_type=F32) * scale).astype(BF16)

    @pl.when(is_prompt)
    def _():
        for s in range(n_seq):
            hs = h[s * seq:(s + 1) * seq, :]
            for t_ref, lo in ((kt_ref, 0), (vt_ref, d)):
                t = _nt_dot(wt_ref[lo:lo + d, :], hs)
                if n_prev:
                    t_ref[s] = t
                else:
                    for j in range(t_ref.shape[1]):
                        t_ref[s, j] = t if j == layer_i else jnp.zeros_like(t)

    @pl.when(jnp.logical_not(is_prompt))
    def _():
        ktl_ref[...] = _nt_dot(wt_ref[:d, :], h).astype(BF16)
        vtl_ref[...] = _nt_dot(wt_ref[d:, :], h).astype(BF16)


def _qkv(x_parts, mod_l, g, w_qkv, prev_caches, *, layer_i, seq, n_prompt, n_tok, latent_seq, tm, scale):
    d = x_parts[0].shape[1]
    n_attn = w_qkv.shape[0]
    npt = n_prompt // tm
    tiles_per_seq = latent_seq // tm
    row = functools.partial(_cond_row, n_prompt_tiles=npt, tiles_per_latent_seq=tiles_per_seq)
    tok = pl.BlockSpec((tm, d), lambda i: (i, 0))
    latent_t = pl.BlockSpec((None, d, tm), lambda i: (jnp.maximum(i - npt, 0) // tiles_per_seq, 0,
                                                     jnp.maximum(i - npt, 0) % tiles_per_seq))
    if prev_caches:
        cache_blk = pl.BlockSpec((tm // seq, None, d, seq), lambda i: (jnp.minimum(i, npt - 1), layer_i, 0, 0))
    else:
        cache_blk = pl.BlockSpec((tm // seq, n_attn, d, seq), lambda i: (jnp.minimum(i, npt - 1), 0, 0, 0))
    cache_shape = jax.ShapeDtypeStruct((n_prompt // seq, n_attn, d, seq), F32)
    latent_shape = jax.ShapeDtypeStruct(((n_tok - n_prompt) // latent_seq, d, latent_seq), BF16)
    n_in = len(x_parts) + 3
    return pl.pallas_call(
        functools.partial(_qkv_kernel, n_x=len(x_parts), n_prev=len(prev_caches), layer_i=layer_i,
                          n_prompt_tiles=npt, scale=scale),
        grid=(n_tok // tm,),
        in_specs=_token_specs(x_parts, tm, npt) + [
            pl.BlockSpec((None, 6, d), lambda i: (row(i), 0, 0)),
            _const_spec((1, d)),
            _layer_spec((d, 3 * d), layer_i),
        ] + [pl.BlockSpec(memory_space=pl.ANY)] * len(prev_caches),
        out_specs=[tok, latent_t, latent_t, cache_blk, cache_blk],
        out_shape=[jax.ShapeDtypeStruct((n_tok, d), BF16), latent_shape, latent_shape, cache_shape, cache_shape],
        input_output_aliases={n_in + j: 3 + j for j in range(len(prev_caches))},
        scratch_shapes=[pltpu.VMEM((2 * d, d), BF16)],
        compiler_params=pltpu.CompilerParams(
            dimension_semantics=("arbitrary",), vmem_limit_bytes=VMEM_LIMIT),
        name="qkv",
    )(*x_parts, mod_l, g, w_qkv, *prev_caches)


def _nt_dot(a, b):
    return lax.dot_general(a, b, (((1,), (1,)), ((), ())), preferred_element_type=F32)


def _pair_heads(qp, first):
    zero = jnp.zeros_like(qp)
    return jnp.where(first, qp, zero), jnp.where(first, zero, qp)


def _prompt_attn_kernel(q_ref, kt_ref, vt_ref, o_ref):
    seq, d = q_ref.shape
    half = LANE // 2
    first = lax.broadcasted_iota(jnp.int32, (seq, LANE), 1) < half
    ones = jnp.ones((2 * SUBLANE, seq), BF16)

    def scores(p):
        sl = slice(p * LANE, (p + 1) * LANE)
        ktp = kt_ref[sl, :].astype(BF16)
        return [jnp.dot(qm, ktp, preferred_element_type=F32) for qm in _pair_heads(q_ref[:, sl], first)]

    def weights(pair_scores):
        return [jnp.exp(s - jnp.max(s, axis=-1, keepdims=True)).astype(BF16) for s in pair_scores]

    def finish(p, pair_weights):
        sl = slice(p * LANE, (p + 1) * LANE)
        vtp = jnp.concatenate([vt_ref[sl, :].astype(BF16), ones], axis=0)
        outs = []
        for e in pair_weights:
            ot = _nt_dot(vtp, e)
            outs.append(ot[:LANE, :] / ot[LANE:LANE + 1, :])
        ot = jnp.concatenate([outs[0][:half, :], outs[1][half:, :]], axis=0)
        o_ref[:, sl] = ot.T.astype(BF16)

    n_pairs = d // LANE
    s_next, e_prev = scores(0), None
    for p in range(n_pairs + 1):
        s_cur, s_next = s_next, (scores(p + 1) if p + 1 < n_pairs else None)
        e_cur = weights(s_cur) if p < n_pairs else None
        if e_prev is not None:
            finish(p - 1, e_prev)
        e_prev = e_cur


def _prompt_attn(q, cache_kt, cache_vt, *, layer_i, n_prompt, seq):
    d = q.shape[1]
    tok = pl.BlockSpec((seq, d), lambda b: (b, 0))
    feat = pl.BlockSpec((None, None, d, seq), lambda b: (b, layer_i, 0, 0))
    return pl.pallas_call(
        _prompt_attn_kernel,
        grid=(n_prompt // seq,),
        in_specs=[tok, feat, feat],
        out_specs=tok,
        out_shape=jax.ShapeDtypeStruct((n_prompt, d), BF16),
        compiler_params=pltpu.CompilerParams(
            dimension_semantics=("arbitrary",), vmem_limit_bytes=VMEM_LIMIT),
        name="prompt_attn",
    )(q, cache_kt, cache_vt)


def _row_start(r, rows):
    kr = min(WIN_ROWS, rows)
    return min(max(r - kr // 2, 0), rows - kr)


def _latent_block_plan(rows):
    kr = min(WIN_ROWS, rows)
    plan = []
    for qb in range(rows // ROWS_PER_BLOCK):
        starts = [_row_start(r, rows) for r in range(qb * ROWS_PER_BLOCK, (qb + 1) * ROWS_PER_BLOCK)]
        plan.append((min(starts) // ROWS_PER_BLOCK, (max(starts) + kr - 1) // ROWS_PER_BLOCK + 1))
    return plan


def _build_pair_bias(rpb_ref, tile_ref, bias_ref, head0, *, plan, rows):
    kr = min(WIN_ROWS, rows)
    shape = (GRID_W, LANE)
    qc = lax.broadcasted_iota(jnp.int32, shape, 0)
    lane = lax.broadcasted_iota(jnp.int32, shape, 1)
    kc = lane & (GRID_W - 1)
    delta = kc - qc + (WIN_COLS - 1)
    col_start = jnp.clip(qc - WIN_COLS // 2, 0, GRID_W - WIN_COLS)
    in_window = (kc >= col_start) & (kc < col_start + WIN_COLS)
    left = lane < GRID_W
    neg = jnp.full(shape, -jnp.inf, F32)
    for hh in range(2):
        base = (head0 + hh) * (N_DROW * N_DCOL)
        for dr in range(N_DROW):
            w = neg
            for dc in range(N_DCOL):
                w = jnp.where(delta == dc, rpb_ref[base + dr * N_DCOL + dc], w)
            tile_ref[dr] = jnp.where(in_window, w, neg)
        t = 0
        for qb, (lo, hi) in enumerate(plan):
            for kb in range(lo, hi):
                for a in range(ROWS_PER_BLOCK):
                    r = qb * ROWS_PER_BLOCK + a
                    start = _row_start(r, rows)
                    for c in range(ATT_BLOCK // LANE):
                        halves = []
                        for jj in range(LANE // GRID_W):
                            k_row = kb * ROWS_PER_BLOCK + c * (LANE // GRID_W) + jj
                            ok = start <= k_row < start + kr
                            halves.append(tile_ref[k_row - r + WIN_ROWS - 1] if ok else None)
                        if halves[0] is None and halves[1] is None:
                            val = neg
                        else:
                            val = jnp.where(left, neg if halves[0] is None else halves[0],
                                            neg if halves[1] is None else halves[1])
                        bias_ref[hh, t, a * GRID_W:(a + 1) * GRID_W, c * LANE:(c + 1) * LANE] = val
                t += 1


def _latent_attn_kernel(rpb_ref, q_ref, kt_ref, vt_ref, kct_ref, vct_ref, wup_ref, wdn_ref, o_ref, wup_o_ref,
                        wdn_o_ref, tile_ref, bias_ref, *, plan, rows, n_batch, n_chunks):
    _cast_chunks(pl.program_id(0) * n_batch + pl.program_id(1), n_chunks, (wup_ref, wdn_ref),
                 (wup_o_ref, wdn_o_ref))

    @pl.when(pl.program_id(1) == 0)
    def _():
        _build_pair_bias(rpb_ref, tile_ref, bias_ref, 2 * pl.program_id(0), plan=plan, rows=rows)

    half = LANE // 2
    first = lax.broadcasted_iota(jnp.int32, (ATT_BLOCK, LANE), 1) < half
    kct = kct_ref[...].astype(BF16)
    vct = jnp.concatenate([vct_ref[...].astype(BF16), jnp.ones((2 * SUBLANE, kct.shape[1]), BF16)], axis=0)
    vt = jnp.concatenate([vt_ref[...], jnp.ones((2 * SUBLANE, vt_ref.shape[1]), BF16)], axis=0)
    bias_starts = [sum(hi - lo for lo, hi in plan[:qb]) for qb in range(len(plan))]

    def scores(qb):
        lo, hi = plan[qb]
        keys = slice(lo * ATT_BLOCK, hi * ATT_BLOCK)
        out = []
        for hh, qm in enumerate(_pair_heads(q_ref[qb * ATT_BLOCK:(qb + 1) * ATT_BLOCK, :], first)):
            bias = jnp.concatenate([bias_ref[hh, bias_starts[qb] + j] for j in range(hi - lo)], axis=1)
            out.append((jnp.dot(qm, kt_ref[:, keys], preferred_element_type=F32) + bias,
                        jnp.dot(qm, kct, preferred_element_type=F32)))
        return out

    def weights(block_scores):
        out = []
        for s_loc, s_ctx in block_scores:
            m = jnp.maximum(jnp.max(s_loc, axis=-1, keepdims=True), jnp.max(s_ctx, axis=-1, keepdims=True))
            out.append((jnp.exp(s_loc - m).astype(BF16), jnp.exp(s_ctx - m).astype(BF16)))
        return out

    def finish(qb, block_weights):
        lo, hi = plan[qb]
        keys = slice(lo * ATT_BLOCK, hi * ATT_BLOCK)
        outs = []
        for e_loc, e_ctx in block_weights:
            ot = _nt_dot(vt[:, keys], e_loc) + _nt_dot(vct, e_ctx)
            outs.append(ot[:LANE, :] / ot[LANE:LANE + 1, :])
        ot = jnp.concatenate([outs[0][:half, :], outs[1][half:, :]], axis=0)
        o_ref[qb * ATT_BLOCK:(qb + 1) * ATT_BLOCK, :] = ot.T.astype(BF16)

    pending = scores(0)
    for qb in range(len(plan)):
        upcoming = scores(qb + 1) if qb + 1 < len(plan) else None
        finish(qb, weights(pending))
        pending = upcoming


def _latent_attn(rpb_flat, q, kt, vt, cache_kt, cache_vt, mlp_weights, *, layer_i, layer, n_prompt, latent_seq):
    d = q.shape[1]
    n_batch, _, _, past = cache_kt.shape
    rows = latent_seq // GRID_W
    plan = _latent_block_plan(rows)
    n_bias = sum(hi - lo for lo, hi in plan)
    first_latent = n_prompt // latent_seq
    n_steps = (d // LANE) * n_batch
    tok_q = pl.BlockSpec((latent_seq, LANE), lambda p, b: (first_latent + b, p))
    feat = pl.BlockSpec((None, LANE, latent_seq), lambda p, b: (b, p, 0))
    ctx = pl.BlockSpec((None, None, LANE, past), lambda p, b: (b, layer_i, p, 0))
    cast_in, cast_out, cast_shapes = _cast_specs(mlp_weights, layer, n_steps, lambda p, b: p * n_batch + b)
    return pl.pallas_call(
        functools.partial(_latent_attn_kernel, plan=plan, rows=rows, n_batch=n_batch, n_chunks=n_steps),
        grid=(d // LANE, n_batch),
        in_specs=[pl.BlockSpec(memory_space=pltpu.SMEM), tok_q, feat, feat, ctx, ctx] + cast_in,
        out_specs=[pl.BlockSpec((latent_seq, LANE), lambda p, b: (b, p))] + cast_out,
        out_shape=[jax.ShapeDtypeStruct((n_batch * latent_seq, d), BF16)] + cast_shapes,
        scratch_shapes=[pltpu.VMEM((N_DROW, GRID_W, LANE), F32),
                        pltpu.VMEM((2, n_bias, ATT_BLOCK, ATT_BLOCK), F32)],
        compiler_params=pltpu.CompilerParams(
            dimension_semantics=("arbitrary", "arbitrary"), vmem_limit_bytes=VMEM_LIMIT),
        name="latent_attn",
    )(rpb_flat, q, kt, vt, cache_kt, cache_vt, *mlp_weights)


def _conv_kernel(x_ref, mod_ref, g_ref, w1_ref, wdw_ref, bdw_ref, lng_ref, lnb_ref, wup_ref, wdn_ref, a_ref,
                 wup_o_ref, wdn_o_ref, pad_ref, *, n_prompt_tiles, prompt_seq, rows_per_step, n_chunks):
    _cast_chunks(pl.program_id(0), n_chunks, (wup_ref, wdn_ref), (wup_o_ref, wdn_o_ref))
    tm, d = x_ref.shape
    h = _rms_mod(x_ref[...], g_ref[...], mod_ref[0:1, :], mod_ref[1:2, :])
    ag = jnp.dot(h.astype(BF16), w1_ref[...], preferred_element_type=F32)
    u = ag[:, :d] * jax.nn.sigmoid(ag[:, d:])
    first_tap = CONV_PAD - CONV_WIDTH // 2
    rs = rows_per_step

    def conv(seq):
        stride = seq + 2 * CONV_PAD
        for s in range(tm // seq):
            base = s * stride
            pad_ref[base:base + CONV_PAD, :] = jnp.zeros((CONV_PAD, d), F32)
            pad_ref[base + CONV_PAD:base + CONV_PAD + seq, :] = u[s * seq:(s + 1) * seq, :]
            pad_ref[base + CONV_PAD + seq:base + stride, :] = jnp.zeros((CONV_PAD, d), F32)
        for s in range(tm // seq):
            base = s * stride

            def step(ci, carry, base=base, s=s):
                r0 = pl.multiple_of(ci * rs, rs)
                parts = []
                for lc in range(d // LANE):
                    ls = slice(lc * LANE, (lc + 1) * LANE)
                    win = pad_ref[pl.ds(base + r0, rs + 2 * CONV_PAD), ls]
                    acc = jnp.broadcast_to(bdw_ref[:, ls], (rs, LANE))
                    for b in range(SUBLANE):
                        vb = None
                        for o in range(b, CONV_WIDTH + first_tap, SUBLANE):
                            t = o - first_tap
                            if t < 0:
                                continue
                            term = win[o - b:o - b + rs + SUBLANE, :] * wdw_ref[t:t + 1, ls]
                            vb = term if vb is None else vb + term
                        if b:
                            vb = pltpu.roll(vb, rs + SUBLANE - b, axis=0)
                        acc = acc + vb[:rs, :]
                    parts.append(acc)
                acc = jnp.concatenate(parts, axis=1)
                mu = jnp.mean(acc, axis=-1, keepdims=True)
                xc = acc - mu
                y = xc * lax.rsqrt(jnp.mean(xc * xc, axis=-1, keepdims=True) + LN_EPS)
                y = y * lng_ref[...] + lnb_ref[...]
                a_ref[pl.ds(s * seq + r0, rs), :] = (y * jax.nn.sigmoid(y)).astype(BF16)
                return carry

            lax.fori_loop(0, seq // rs, step, 0)

    is_prompt = pl.program_id(0) < n_prompt_tiles
    pl.when(is_prompt)(lambda: conv(prompt_seq))
    pl.when(jnp.logical_not(is_prompt))(lambda: conv(tm))


def _conv_front(x, mod_l, g, w_pw1, w_dw, b_dw, ln_g, ln_b, mlp_weights, *, layer_i, layer, n_prompt, prompt_seq,
                latent_seq):
    n_tok, d = x.shape
    tm = latent_seq
    npt = n_prompt // tm
    n_chunks = 8
    assert n_tok // tm >= n_chunks
    row = functools.partial(_cond_row, n_prompt_tiles=npt, tiles_per_latent_seq=1)
    tok = pl.BlockSpec((tm, d), lambda i: (i, 0))
    pad_rows = (tm // prompt_seq) * (prompt_seq + 2 * CONV_PAD)
    cast_in, cast_out, cast_shapes = _cast_specs(mlp_weights, layer, n_chunks, lambda i: i)
    return pl.pallas_call(
        functools.partial(_conv_kernel, n_prompt_tiles=npt, prompt_seq=prompt_seq, rows_per_step=128,
                          n_chunks=n_chunks),
        grid=(n_tok // tm,),
        in_specs=[
            tok,
            pl.BlockSpec((None, 6, d), lambda i: (row(i), 0, 0)),
            _const_spec((1, d)),
            _layer_spec((d, 2 * d), layer_i),
            _const_spec((CONV_WIDTH, d)),
            _const_spec((1, d)),
            _const_spec((1, d)),
            _const_spec((1, d)),
        ] + cast_in,
        out_specs=[tok] + cast_out,
        out_shape=[jax.ShapeDtypeStruct((n_tok, d), BF16)] + cast_shapes,
        scratch_shapes=[pltpu.VMEM((pad_rows, d), F32)],
        compiler_params=pltpu.CompilerParams(
            dimension_semantics=("arbitrary",), vmem_limit_bytes=VMEM_LIMIT),
        name="conv_front",
    )(x, mod_l, g, w_pw1, w_dw, b_dw, ln_g, ln_b, *mlp_weights)


def _post_kernel(*refs, n_x, n_a, n_out, n_prompt_tiles, ff_chunk, final):
    x_refs, a_refs = refs[:n_x], refs[n_x:n_x + n_a]
    mod_ref, g_ref, wp_ref, wup_ref, wdn_ref, fg_ref = refs[n_x + n_a:n_x + n_a + 6]
    o_refs = refs[n_x + n_a + 6:]
    is_prompt = pl.program_id(0) < n_prompt_tiles
    y = jnp.dot(_token_tile(a_refs, is_prompt), wp_ref[...], preferred_element_type=F32)
    x1 = _token_tile(x_refs, is_prompt) + mod_ref[2:3, :] * y
    h = _rms_mod(x1, g_ref[...], mod_ref[3:4, :], mod_ref[4:5, :]).astype(BF16)
    acc = jnp.zeros_like(x1)
    for c in range(wup_ref.shape[1] // ff_chunk):
        sl = slice(c * ff_chunk, (c + 1) * ff_chunk)
        u = jnp.maximum(jnp.dot(h, wup_ref[:, sl], preferred_element_type=F32), 0.0)
        acc = acc + jnp.dot((u * u).astype(BF16), wdn_ref[sl, :], preferred_element_type=F32)
    out = x1 + mod_ref[5:6, :] * acc
    if final:
        out = out * lax.rsqrt(jnp.mean(out * out, axis=-1, keepdims=True) + RMS_EPS) * fg_ref[...]
    if n_out == 1:
        o_refs[0][...] = out
    else:
        @pl.when(is_prompt)
        def _():
            o_refs[0][...] = out

        @pl.when(jnp.logical_not(is_prompt))
        def _():
            o_refs[1][...] = out


def _post(x_parts, a_parts, mod_l, g, w_proj, w_up, w_down, final_g, *, proj_i, n_prompt, n_tok, latent_seq, tm,
          final):
    d = x_parts[0].shape[1]
    d_ff = w_up.shape[1]
    npt = n_prompt // tm
    row = functools.partial(_cond_row, n_prompt_tiles=npt, tiles_per_latent_seq=latent_seq // tm)
    if final:
        out_shape = [jax.ShapeDtypeStruct((n_prompt, d), F32), jax.ShapeDtypeStruct((n_tok - n_prompt, d), F32)]
    else:
        out_shape = [jax.ShapeDtypeStruct((n_tok, d), F32)]
    return pl.pallas_call(
        functools.partial(_post_kernel, n_x=len(x_parts), n_a=len(a_parts), n_out=len(out_shape),
                          n_prompt_tiles=npt, ff_chunk=1024, final=final),
        grid=(n_tok // tm,),
        in_specs=_token_specs(x_parts, tm, npt) + _token_specs(a_parts, tm, npt) + [
            pl.BlockSpec((None, 6, d), lambda i: (row(i), 0, 0)),
            _const_spec((1, d)),
            _layer_spec((d, d), proj_i),
            _const_spec((d, d_ff)),
            _const_spec((d_ff, d)),
            _const_spec((1, d)),
        ],
        out_specs=_token_specs(out_shape, tm, npt),
        out_shape=out_shape,
        compiler_params=pltpu.CompilerParams(
            dimension_semantics=("arbitrary",), vmem_limit_bytes=VMEM_LIMIT),
        name="post_mlp",
    )(*x_parts, *a_parts, mod_l, g, w_proj, w_up, w_down, final_g)


def kernel(x_prompt, x_sample, cache_k, cache_v, c, c_ctx, norm_g, w_ada, b_ada, w_qkv, w_o, rpb,
           w_pw1, w_dw, b_dw, conv_ln_g, conv_ln_b, w_pw2, w_up, w_down, final_g):
    batch, seq, d = x_prompt.shape
    dec_batch, dec_seq, _ = x_sample.shape
    depth = w_ada.shape[0]
    n_attn = w_qkv.shape[0]
    past = cache_k.shape[2]
    n_prompt = batch * seq
    n_tok = n_prompt + dec_batch * dec_seq
    head_dim = d // N_HEADS
    assert d == N_HEADS * head_dim and 2 * head_dim == LANE and LANE == 2 * GRID_W
    assert seq == ATT_BLOCK and past == ATT_BLOCK and dec_seq % ATT_BLOCK == 0
    assert rpb.shape[1:] == (N_HEADS, N_DROW, N_DCOL)
    assert 1 + dec_batch <= N_COND_ROWS
    tm = 512
    dims = dict(n_prompt=n_prompt, n_tok=n_tok, latent_seq=dec_seq)

    x_parts = [x_prompt.reshape(n_prompt, d), x_sample.reshape(dec_batch * dec_seq, d)]
    cond = jnp.concatenate([c_ctx[None, :], c, jnp.zeros((N_COND_ROWS - 1 - dec_batch, d), F32)], axis=0)
    mod = _adaln(cond, w_ada, b_ada).reshape(depth, N_COND_ROWS, 6, d)
    ckt = jnp.transpose(cache_k, (0, 1, 3, 4, 2)).reshape(dec_batch, n_attn, d, past)
    cvt = jnp.transpose(cache_v, (0, 1, 3, 4, 2)).reshape(dec_batch, n_attn, d, past)
    final_g2 = final_g.reshape(1, d)
    w_qkv_b, w_o_b, w_pw1_b, w_pw2_b = [w.astype(BF16) for w in (w_qkv, w_o, w_pw1, w_pw2)]
    mlp_weights = (w_up, w_down)

    new_caches = ()
    for l in range(depth):
        i = l // 2
        mod_l = mod[l]
        g1 = norm_g[l, 0].reshape(1, d)
        g2 = norm_g[l, 1].reshape(1, d)
        if l % 2 == 0:
            q, kt, vt, *new_caches = _qkv(x_parts, mod_l, g1, w_qkv_b, new_caches, layer_i=i, seq=seq, tm=tm,
                                          scale=head_dim ** -0.5, **dims)
            o_p = _prompt_attn(q, *new_caches, layer_i=i, n_prompt=n_prompt, seq=seq)
            o_s, w_up_l, w_down_l = _latent_attn(rpb[i].reshape(-1), q, kt, vt, ckt, cvt, mlp_weights, layer_i=i,
                                                 layer=l, n_prompt=n_prompt, latent_seq=dec_seq)
            a_parts = [o_p, o_s]
            w_proj = w_o_b
        else:
            assert len(x_parts) == 1
            a, w_up_l, w_down_l = _conv_front(x_parts[0], mod_l, g1, w_pw1_b, w_dw[i], b_dw[i].reshape(1, d),
                                              conv_ln_g[i].reshape(1, d), conv_ln_b[i].reshape(1, d), mlp_weights,
                                              layer_i=i, layer=l, n_prompt=n_prompt, prompt_seq=seq,
                                              latent_seq=dec_seq)
            a_parts = [a]
            w_proj = w_pw2_b
        x_parts = _post(x_parts, a_parts, mod_l, g2, w_proj, w_up_l, w_down_l, final_g2, proj_i=i, tm=tm,
                        final=(l == depth - 1), **dims)

    y_prompt, y_sample = x_parts
    new_k, new_v = [jnp.transpose(t.reshape(batch, n_attn, N_HEADS, head_dim, seq), (0, 1, 4, 2, 3))
                    for t in new_caches]
    return (y_prompt.reshape(batch, seq, d), y_sample.reshape(dec_batch, dec_seq, d), new_k, new_v)
```

```python
import functools

import numpy as np
import jax
import jax.numpy as jnp
from jax import lax
from jax.experimental import pallas as pl
from jax.experimental.pallas import tpu as pltpu

F32 = jnp.float32
BF16 = jnp.bfloat16

RMS_EPS = 1e-6
LN_EPS = 1e-5
N_HEADS = 16
GRID_W = 64
WIN_ROWS = 8
WIN_COLS = 16
N_DROW = 2 * WIN_ROWS - 1
N_DCOL = 2 * WIN_COLS - 1
CONV_WIDTH = 31
CONV_PAD = 16
N_COND_ROWS = 8

LANE = 128
SUBLANE = 8
ATT_BLOCK = 256
ROWS_PER_BLOCK = ATT_BLOCK // GRID_W
VMEM_LIMIT = 56 * 1024 * 1024


def _rms_mod(xf, g, shift, scale):
    y = xf * lax.rsqrt(jnp.mean(xf * xf, axis=-1, keepdims=True) + RMS_EPS)
    return (y * g) * (1.0 + scale) + shift


def _cond_row(i, n_prompt_tiles, tiles_per_latent_seq):
    return jnp.where(i < n_prompt_tiles, 0, 1 + (i - n_prompt_tiles) // tiles_per_latent_seq)


def _const_spec(shape):
    nd = len(shape)
    return pl.BlockSpec(shape, lambda *_: (0,) * nd, pipeline_mode=pl.Buffered(1))


def _layer_spec(shape, layer):
    nd = len(shape)
    return pl.BlockSpec((None,) + tuple(shape), lambda *_: (layer,) + (0,) * nd, pipeline_mode=pl.Buffered(1))


def _token_specs(parts, tm, n_prompt_tiles):
    d = parts[0].shape[1]
    if len(parts) == 1:
        return [pl.BlockSpec((tm, d), lambda i: (i, 0))]
    return [pl.BlockSpec((tm, d), lambda i: (jnp.minimum(i, n_prompt_tiles - 1), 0)),
            pl.BlockSpec((tm, d), lambda i: (jnp.maximum(i - n_prompt_tiles, 0), 0))]


def _token_tile(refs, is_prompt):
    if len(refs) == 1:
        return refs[0][...]
    return jnp.where(is_prompt, refs[0][...], refs[1][...])


def _cast_specs(weights, layer, n_chunks, step_of):
    in_specs, out_specs, out_shapes = [], [], []
    for w in weights:
        _, r, c = w.shape
        chunk = lambda *idx: jnp.minimum(step_of(*idx), n_chunks - 1)
        in_specs.append(pl.BlockSpec((None, r // n_chunks, c), lambda *idx, chunk=chunk: (layer, chunk(*idx), 0)))
        out_specs.append(pl.BlockSpec((r // n_chunks, c), lambda *idx, chunk=chunk: (chunk(*idx), 0)))
        out_shapes.append(jax.ShapeDtypeStruct((r, c), BF16))
    return in_specs, out_specs, out_shapes


def _cast_chunks(step, n_chunks, w_refs, o_refs):
    @pl.when(step < n_chunks)
    def _():
        for w_ref, o_ref in zip(w_refs, o_refs):
            o_ref[...] = w_ref[...].astype(BF16)


def _adaln_kernel(cond_ref, w_ref, b_ref, o_ref):
    s = cond_ref[...]
    s = s * jax.nn.sigmoid(s)
    o_ref[0] = jnp.dot(s.astype(BF16), w_ref[0].astype(BF16), preferred_element_type=F32) + b_ref[0]


def _adaln(cond, w_ada, b_ada):
    depth, d, n6 = w_ada.shape
    tn = n6 // 4
    return pl.pallas_call(
        _adaln_kernel,
        grid=(depth, n6 // tn),
        in_specs=[
            pl.BlockSpec((N_COND_ROWS, d), lambda l, j: (0, 0)),
            pl.BlockSpec((1, d, tn), lambda l, j: (l, 0, j)),
            pl.BlockSpec((1, 1, tn), lambda l, j: (l, 0, j)),
        ],
        out_specs=pl.BlockSpec((1, N_COND_ROWS, tn), lambda l, j: (l, 0, j)),
        out_shape=jax.ShapeDtypeStruct((depth, N_COND_ROWS, n6), F32),
        compiler_params=pltpu.CompilerParams(
            dimension_semantics=("arbitrary", "arbitrary"), vmem_limit_bytes=VMEM_LIMIT),
        name="adaln",
    )(cond, w_ada, b_ada.reshape(depth, 1, n6))


def _qkv_kernel(*refs, n_x, n_prev, layer_i, n_prompt_tiles, scale):
    x_refs = refs[:n_x]
    mod_ref, g_ref, w_ref = refs[n_x:n_x + 3]
    q_ref, ktl_ref, vtl_ref, kt_ref, vt_ref, ktp_ref, vtp_ref, wt_ref = refs[n_x + 3 + n_prev:]
    d = q_ref.shape[1]
    n_seq, seq = kt_ref.shape[0], kt_ref.shape[-1]

    @pl.when(pl.program_id(0) == 0)
    def _():
        for r in range(0, 2 * d, ATT_BLOCK):
            for c in range(0, d, ATT_BLOCK):
                wt_ref[r:r + ATT_BLOCK, c:c + ATT_BLOCK] = w_ref[c:c + ATT_BLOCK, d + r:d + r + ATT_BLOCK].T

    is_prompt = pl.program_id(0) < n_prompt_tiles
    h = _rms_mod(_token_tile(x_refs, is_prompt), g_ref[...], mod_ref[0:1, :], mod_ref[1:2, :]).astype(BF16)
    q_ref[...] = (jnp.dot(h, w_ref[:, :d], preferred_element_type=F32) * scale).astype(BF16)

    @pl.when(is_prompt)
    def _():
        for s in range(n_seq):
            hs = h[s * seq:(s + 1) * seq, :]
            for t_ref, tb_ref, lo in ((kt_ref, ktp_ref, 0), (vt_ref, vtp_ref, d)):
                t = _nt_dot(wt_ref[lo:lo + d, :], hs)
                tb_ref[s] = t.astype(BF16)
                if n_prev:
                    t_ref[s] = t
                else:
                    for j in range(t_ref.shape[1]):
                        t_ref[s, j] = t if j == layer_i else jnp.zeros_like(t)

    @pl.when(jnp.logical_not(is_prompt))
    def _():
        ktl_ref[...] = _nt_dot(wt_ref[:d, :], h).astype(BF16)
        vtl_ref[...] = _nt_dot(wt_ref[d:, :], h).astype(BF16)


def _qkv(x_parts, mod_l, g, w_qkv, prev_caches, *, layer_i, seq, n_prompt, n_tok, latent_seq, tm, scale):
    d = x_parts[0].shape[1]
    n_attn = w_qkv.shape[0]
    npt = n_prompt // tm
    tiles_per_seq = latent_seq // tm
    row = functools.partial(_cond_row, n_prompt_tiles=npt, tiles_per_latent_seq=tiles_per_seq)
    tok = pl.BlockSpec((tm, d), lambda i: (i, 0))
    latent_t = pl.BlockSpec((None, d, tm), lambda i: (jnp.maximum(i - npt, 0) // tiles_per_seq, 0,
                                                     jnp.maximum(i - npt, 0) % tiles_per_seq))
    if prev_caches:
        cache_blk = pl.BlockSpec((tm // seq, None, d, seq), lambda i: (jnp.minimum(i, npt - 1), layer_i, 0, 0))
    else:
        cache_blk = pl.BlockSpec((tm // seq, n_attn, d, seq), lambda i: (jnp.minimum(i, npt - 1), 0, 0, 0))
    cache_shape = jax.ShapeDtypeStruct((n_prompt // seq, n_attn, d, seq), F32)
    latent_shape = jax.ShapeDtypeStruct(((n_tok - n_prompt) // latent_seq, d, latent_seq), BF16)
    prompt_t = pl.BlockSpec((tm // seq, d, seq), lambda i: (jnp.minimum(i, npt - 1), 0, 0))
    prompt_shape = jax.ShapeDtypeStruct((n_prompt // seq, d, seq), BF16)
    n_in = len(x_parts) + 3
    return pl.pallas_call(
        functools.partial(_qkv_kernel, n_x=len(x_parts), n_prev=len(prev_caches), layer_i=layer_i,
                          n_prompt_tiles=npt, scale=scale),
        grid=(n_tok // tm,),
        in_specs=_token_specs(x_parts, tm, npt) + [
            pl.BlockSpec((None, 6, d), lambda i: (row(i), 0, 0)),
            _const_spec((1, d)),
            _layer_spec((d, 3 * d), layer_i),
        ] + [pl.BlockSpec(memory_space=pl.ANY)] * len(prev_caches),
        out_specs=[tok, latent_t, latent_t, cache_blk, cache_blk, prompt_t, prompt_t],
        out_shape=[jax.ShapeDtypeStruct((n_tok, d), BF16), latent_shape, latent_shape, cache_shape, cache_shape,
                   prompt_shape, prompt_shape],
        input_output_aliases={n_in + j: 3 + j for j in range(len(prev_caches))},
        scratch_shapes=[pltpu.VMEM((2 * d, d), BF16)],
        compiler_params=pltpu.CompilerParams(
            dimension_semantics=("arbitrary",), vmem_limit_bytes=VMEM_LIMIT),
        name="qkv",
    )(*x_parts, mod_l, g, w_qkv, *prev_caches)


def _nt_dot(a, b):
    return lax.dot_general(a, b, (((1,), (1,)), ((), ())), preferred_element_type=F32)


def _pair_heads(qp, first):
    zero = jnp.zeros_like(qp)
    return jnp.where(first, qp, zero), jnp.where(first, zero, qp)


def _prompt_attn_kernel(q_ref, kt_ref, vt_ref, o_ref):
    n_seq, d, seq = kt_ref.shape
    half = LANE // 2
    first = lax.broadcasted_iota(jnp.int32, (seq, LANE), 1) < half
    ones = jnp.ones((2 * SUBLANE, seq), BF16)

    def scores(s, p):
        sl = slice(p * LANE, (p + 1) * LANE)
        ktp = kt_ref[s, sl, :]
        return [jnp.dot(qm, ktp, preferred_element_type=F32)
                for qm in _pair_heads(q_ref[s * seq:(s + 1) * seq, sl], first)]

    def weights(pair_scores):
        return [jnp.exp(sc - jnp.max(sc, axis=-1, keepdims=True)).astype(BF16) for sc in pair_scores]

    def finish(s, p, pair_weights):
        sl = slice(p * LANE, (p + 1) * LANE)
        vtp = jnp.concatenate([vt_ref[s, sl, :], ones], axis=0)
        outs = []
        for e in pair_weights:
            ot = _nt_dot(vtp, e)
            outs.append(ot[:LANE, :] / ot[LANE:LANE + 1, :])
        ot = jnp.concatenate([outs[0][:half, :], outs[1][half:, :]], axis=0)
        o_ref[s * seq:(s + 1) * seq, sl] = ot.T.astype(BF16)

    chains = [(s, p) for s in range(n_seq) for p in range(d // LANE)]
    s_next, e_prev = scores(*chains[0]), None
    for n in range(len(chains) + 1):
        s_cur, s_next = s_next, (scores(*chains[n + 1]) if n + 1 < len(chains) else None)
        e_cur = weights(s_cur) if n < len(chains) else None
        if e_prev is not None:
            finish(*chains[n - 1], e_prev)
        e_prev = e_cur


def _prompt_attn(q, kt, vt, *, n_prompt, seq):
    d = q.shape[1]
    n_seq = 2
    tok = pl.BlockSpec((n_seq * seq, d), lambda b: (b, 0))
    feat = pl.BlockSpec((n_seq, d, seq), lambda b: (b, 0, 0))
    return pl.pallas_call(
        _prompt_attn_kernel,
        grid=(n_prompt // (n_seq * seq),),
        in_specs=[tok, feat, feat],
        out_specs=tok,
        out_shape=jax.ShapeDtypeStruct((n_prompt, d), BF16),
        compiler_params=pltpu.CompilerParams(
            dimension_semantics=("arbitrary",), vmem_limit_bytes=VMEM_LIMIT),
        name="prompt_attn",
    )(q, kt, vt)


def _row_start(r, rows):
    kr = min(WIN_ROWS, rows)
    return min(max(r - kr // 2, 0), rows - kr)


def _latent_block_plan(rows):
    kr = min(WIN_ROWS, rows)
    plan = []
    for qb in range(rows // ROWS_PER_BLOCK):
        starts = [_row_start(r, rows) for r in range(qb * ROWS_PER_BLOCK, (qb + 1) * ROWS_PER_BLOCK)]
        plan.append((min(starts) // ROWS_PER_BLOCK, (max(starts) + kr - 1) // ROWS_PER_BLOCK + 1))
    return plan


def _build_pair_bias(rpb_ref, tile_ref, bias_ref, head0, *, plan, rows):
    kr = min(WIN_ROWS, rows)
    shape = (GRID_W, LANE)
    qc = lax.broadcasted_iota(jnp.int32, shape, 0)
    lane = lax.broadcasted_iota(jnp.int32, shape, 1)
    kc = lane & (GRID_W - 1)
    delta = kc - qc + (WIN_COLS - 1)
    col_start = jnp.clip(qc - WIN_COLS // 2, 0, GRID_W - WIN_COLS)
    in_window = (kc >= col_start) & (kc < col_start + WIN_COLS)
    left = lane < GRID_W
    neg = jnp.full(shape, -jnp.inf, F32)
    for hh in range(2):
        base = (head0 + hh) * (N_DROW * N_DCOL)
        for dr in range(N_DROW):
            w = neg
            for dc in range(N_DCOL):
                w = jnp.where(delta == dc, rpb_ref[base + dr * N_DCOL + dc], w)
            tile_ref[dr] = jnp.where(in_window, w, neg)
        t = 0
        for qb, (lo, hi) in enumerate(plan):
            for kb in range(lo, hi):
                for a in range(ROWS_PER_BLOCK):
                    r = qb * ROWS_PER_BLOCK + a
                    start = _row_start(r, rows)
                    for c in range(ATT_BLOCK // LANE):
                        halves = []
                        for jj in range(LANE // GRID_W):
                            k_row = kb * ROWS_PER_BLOCK + c * (LANE // GRID_W) + jj
                            ok = start <= k_row < start + kr
                            halves.append(tile_ref[k_row - r + WIN_ROWS - 1] if ok else None)
                        if halves[0] is None and halves[1] is None:
                            val = neg
                        else:
                            val = jnp.where(left, neg if halves[0] is None else halves[0],
                                            neg if halves[1] is None else halves[1])
                        bias_ref[hh, t, a * GRID_W:(a + 1) * GRID_W, c * LANE:(c + 1) * LANE] = val
                t += 1


def _latent_attn_kernel(rpb_ref, q_ref, kt_ref, vt_ref, kct_ref, vct_ref, wup_ref, wdn_ref, o_ref, wup_o_ref,
                        wdn_o_ref, tile_ref, bias_ref, *, plan, rows, n_batch, n_chunks):
    _cast_chunks(pl.program_id(0) * n_batch + pl.program_id(1), n_chunks, (wup_ref, wdn_ref),
                 (wup_o_ref, wdn_o_ref))

    @pl.when(pl.program_id(1) == 0)
    def _():
        _build_pair_bias(rpb_ref, tile_ref, bias_ref, 2 * pl.program_id(0), plan=plan, rows=rows)

    half = LANE // 2
    first = lax.broadcasted_iota(jnp.int32, (ATT_BLOCK, LANE), 1) < half
    kct = kct_ref[...].astype(BF16)
    vct = jnp.concatenate([vct_ref[...].astype(BF16), jnp.ones((2 * SUBLANE, kct.shape[1]), BF16)], axis=0)
    vt = jnp.concatenate([vt_ref[...], jnp.ones((2 * SUBLANE, vt_ref.shape[1]), BF16)], axis=0)
    bias_starts = [sum(hi - lo for lo, hi in plan[:qb]) for qb in range(len(plan))]

    def scores(qb):
        lo, hi = plan[qb]
        keys = slice(lo * ATT_BLOCK, hi * ATT_BLOCK)
        out = []
        for hh, qm in enumerate(_pair_heads(q_ref[qb * ATT_BLOCK:(qb + 1) * ATT_BLOCK, :], first)):
            bias = jnp.concatenate([bias_ref[hh, bias_starts[qb] + j] for j in range(hi - lo)], axis=1)
            out.append((jnp.dot(qm, kt_ref[:, keys], preferred_element_type=F32) + bias,
                        jnp.dot(qm, kct, preferred_element_type=F32)))
        return out

    def weights(block_scores):
        out = []
        for s_loc, s_ctx in block_scores:
            m = jnp.maximum(jnp.max(s_loc, axis=-1, keepdims=True), jnp.max(s_ctx, axis=-1, keepdims=True))
            out.append((jnp.exp(s_loc - m).astype(BF16), jnp.exp(s_ctx - m).astype(BF16)))
        return out

    def finish(qb, block_weights):
        lo, hi = plan[qb]
        keys = slice(lo * ATT_BLOCK, hi * ATT_BLOCK)
        outs = []
        for e_loc, e_ctx in block_weights:
            ot = _nt_dot(vt[:, keys], e_loc) + _nt_dot(vct, e_ctx)
            outs.append(ot[:LANE, :] / ot[LANE:LANE + 1, :])
        ot = jnp.concatenate([outs[0][:half, :], outs[1][half:, :]], axis=0)
        o_ref[qb * ATT_BLOCK:(qb + 1) * ATT_BLOCK, :] = ot.T.astype(BF16)

    pending = scores(0)
    for qb in range(len(plan)):
        upcoming = scores(qb + 1) if qb + 1 < len(plan) else None
        finish(qb, weights(pending))
        pending = upcoming


def _latent_attn(rpb_flat, q, kt, vt, cache_kt, cache_vt, mlp_weights, *, layer_i, layer, n_prompt, latent_seq):
    d = q.shape[1]
    n_batch, _, _, past = cache_kt.shape
    rows = latent_seq // GRID_W
    plan = _latent_block_plan(rows)
    n_bias = sum(hi - lo for lo, hi in plan)
    first_latent = n_prompt // latent_seq
    n_steps = (d // LANE) * n_batch
    tok_q = pl.BlockSpec((latent_seq, LANE), lambda p, b: (first_latent + b, p))
    feat = pl.BlockSpec((None, LANE, latent_seq), lambda p, b: (b, p, 0))
    ctx = pl.BlockSpec((None, None, LANE, past), lambda p, b: (b, layer_i, p, 0))
    cast_in, cast_out, cast_shapes = _cast_specs(mlp_weights, layer, n_steps, lambda p, b: p * n_batch + b)
    return pl.pallas_call(
        functools.partial(_latent_attn_kernel, plan=plan, rows=rows, n_batch=n_batch, n_chunks=n_steps),
        grid=(d // LANE, n_batch),
        in_specs=[pl.BlockSpec(memory_space=pltpu.SMEM), tok_q, feat, feat, ctx, ctx] + cast_in,
        out_specs=[pl.BlockSpec((latent_seq, LANE), lambda p, b: (b, p))] + cast_out,
        out_shape=[jax.ShapeDtypeStruct((n_batch * latent_seq, d), BF16)] + cast_shapes,
        scratch_shapes=[pltpu.VMEM((N_DROW, GRID_W, LANE), F32),
                        pltpu.VMEM((2, n_bias, ATT_BLOCK, ATT_BLOCK), F32)],
        compiler_params=pltpu.CompilerParams(
            dimension_semantics=("arbitrary", "arbitrary"), vmem_limit_bytes=VMEM_LIMIT),
        name="latent_attn",
    )(rpb_flat, q, kt, vt, cache_kt, cache_vt, *mlp_weights)


def _conv_kernel(x_ref, mod_ref, g_ref, w1_ref, wdw_ref, bdw_ref, lng_ref, lnb_ref, wup_ref, wdn_ref, a_ref,
                 wup_o_ref, wdn_o_ref, pad_ref, *, n_prompt_tiles, prompt_seq, rows_per_step, n_chunks):
    _cast_chunks(pl.program_id(0), n_chunks, (wup_ref, wdn_ref), (wup_o_ref, wdn_o_ref))
    tm, d = x_ref.shape
    h = _rms_mod(x_ref[...], g_ref[...], mod_ref[0:1, :], mod_ref[1:2, :])
    ag = jnp.dot(h.astype(BF16), w1_ref[...], preferred_element_type=F32)
    u = ag[:, :d] * jax.nn.sigmoid(ag[:, d:])
    first_tap = CONV_PAD - CONV_WIDTH // 2
    rs = rows_per_step

    def conv(seq):
        stride = seq + 2 * CONV_PAD
        for s in range(tm // seq):
            base = s * stride
            pad_ref[base:base + CONV_PAD, :] = jnp.zeros((CONV_PAD, d), F32)
            pad_ref[base + CONV_PAD:base + CONV_PAD + seq, :] = u[s * seq:(s + 1) * seq, :]
            pad_ref[base + CONV_PAD + seq:base + stride, :] = jnp.zeros((CONV_PAD, d), F32)
        for s in range(tm // seq):
            base = s * stride

            def step(ci, carry, base=base, s=s):
                r0 = pl.multiple_of(ci * rs, rs)
                parts = []
                for lc in range(d // LANE):
                    ls = slice(lc * LANE, (lc + 1) * LANE)
                    win = pad_ref[pl.ds(base + r0, rs + 2 * CONV_PAD), ls]
                    acc = jnp.broadcast_to(bdw_ref[:, ls], (rs, LANE))
                    for b in range(SUBLANE):
                        vb = None
                        for o in range(b, CONV_WIDTH + first_tap, SUBLANE):
                            t = o - first_tap
                            if t < 0:
                                continue
                            term = win[o - b:o - b + rs + SUBLANE, :] * wdw_ref[t:t + 1, ls]
                            vb = term if vb is None else vb + term
                        if b:
                            vb = pltpu.roll(vb, rs + SUBLANE - b, axis=0)
                        acc = acc + vb[:rs, :]
                    parts.append(acc)
                acc = jnp.concatenate(parts, axis=1)
                mu = jnp.mean(acc, axis=-1, keepdims=True)
                xc = acc - mu
                y = xc * lax.rsqrt(jnp.mean(xc * xc, axis=-1, keepdims=True) + LN_EPS)
                y = y * lng_ref[...] + lnb_ref[...]
                a_ref[pl.ds(s * seq + r0, rs), :] = (y * jax.nn.sigmoid(y)).astype(BF16)
                return carry

            lax.fori_loop(0, seq // rs, step, 0)

    is_prompt = pl.program_id(0) < n_prompt_tiles
    pl.when(is_prompt)(lambda: conv(prompt_seq))
    pl.when(jnp.logical_not(is_prompt))(lambda: conv(tm))


def _conv_front(x, mod_l, g, w_pw1, w_dw, b_dw, ln_g, ln_b, mlp_weights, *, layer_i, layer, n_prompt, prompt_seq,
                latent_seq):
    n_tok, d = x.shape
    tm = latent_seq
    npt = n_prompt // tm
    n_chunks = 8
    assert n_tok // tm >= n_chunks
    row = functools.partial(_cond_row, n_prompt_tiles=npt, tiles_per_latent_seq=1)
    tok = pl.BlockSpec((tm, d), lambda i: (i, 0))
    pad_rows = (tm // prompt_seq) * (prompt_seq + 2 * CONV_PAD)
    cast_in, cast_out, cast_shapes = _cast_specs(mlp_weights, layer, n_chunks, lambda i: i)
    return pl.pallas_call(
        functools.partial(_conv_kernel, n_prompt_tiles=npt, prompt_seq=prompt_seq, rows_per_step=128,
                          n_chunks=n_chunks),
        grid=(n_tok // tm,),
        in_specs=[
            tok,
            pl.BlockSpec((None, 6, d), lambda i: (row(i), 0, 0)),
            _const_spec((1, d)),
            _layer_spec((d, 2 * d), layer_i),
            _const_spec((CONV_WIDTH, d)),
            _const_spec((1, d)),
            _const_spec((1, d)),
            _const_spec((1, d)),
        ] + cast_in,
        out_specs=[tok] + cast_out,
        out_shape=[jax.ShapeDtypeStruct((n_tok, d), BF16)] + cast_shapes,
        scratch_shapes=[pltpu.VMEM((pad_rows, d), F32)],
        compiler_params=pltpu.CompilerParams(
            dimension_semantics=("arbitrary",), vmem_limit_bytes=VMEM_LIMIT),
        name="conv_front",
    )(x, mod_l, g, w_pw1, w_dw, b_dw, ln_g, ln_b, *mlp_weights)


def _post_kernel(*refs, n_x, n_a, n_out, n_prompt_tiles, ff_chunk, final):
    x_refs, a_refs = refs[:n_x], refs[n_x:n_x + n_a]
    mod_ref, g_ref, wp_ref, wup_ref, wdn_ref, fg_ref = refs[n_x + n_a:n_x + n_a + 6]
    o_refs = refs[n_x + n_a + 6:]
    is_prompt = pl.program_id(0) < n_prompt_tiles
    y = jnp.dot(_token_tile(a_refs, is_prompt), wp_ref[...], preferred_element_type=F32)
    x1 = _token_tile(x_refs, is_prompt) + mod_ref[2:3, :] * y
    h = _rms_mod(x1, g_ref[...], mod_ref[3:4, :], mod_ref[4:5, :]).astype(BF16)
    acc = jnp.zeros_like(x1)
    for c in range(wup_ref.shape[1] // ff_chunk):
        sl = slice(c * ff_chunk, (c + 1) * ff_chunk)
        u = jnp.maximum(jnp.dot(h, wup_ref[:, sl], preferred_element_type=F32), 0.0)
        acc = acc + jnp.dot((u * u).astype(BF16), wdn_ref[sl, :], preferred_element_type=F32)
    out = x1 + mod_ref[5:6, :] * acc
    if final:
        out = out * lax.rsqrt(jnp.mean(out * out, axis=-1, keepdims=True) + RMS_EPS) * fg_ref[...]
    if n_out == 1:
        o_refs[0][...] = out
    else:
        @pl.when(is_prompt)
        def _():
            o_refs[0][...] = out

        @pl.when(jnp.logical_not(is_prompt))
        def _():
            o_refs[1][...] = out


def _post(x_parts, a_parts, mod_l, g, w_proj, w_up, w_down, final_g, *, proj_i, n_prompt, n_tok, latent_seq, tm,
          final):
    d = x_parts[0].shape[1]
    d_ff = w_up.shape[1]
    npt = n_prompt // tm
    row = functools.partial(_cond_row, n_prompt_tiles=npt, tiles_per_latent_seq=latent_seq // tm)
    if final:
        out_shape = [jax.ShapeDtypeStruct((n_prompt, d), F32), jax.ShapeDtypeStruct((n_tok - n_prompt, d), F32)]
    else:
        out_shape = [jax.ShapeDtypeStruct((n_tok, d), F32)]
    return pl.pallas_call(
        functools.partial(_post_kernel, n_x=len(x_parts), n_a=len(a_parts), n_out=len(out_shape),
                          n_prompt_tiles=npt, ff_chunk=1024, final=final),
        grid=(n_tok // tm,),
        in_specs=_token_specs(x_parts, tm, npt) + _token_specs(a_parts, tm, npt) + [
            pl.BlockSpec((None, 6, d), lambda i: (row(i), 0, 0)),
            _const_spec((1, d)),
            _layer_spec((d, d), proj_i),
            _const_spec((d, d_ff)),
            _const_spec((d_ff, d)),
            _const_spec((1, d)),
        ],
        out_specs=_token_specs(out_shape, tm, npt),
        out_shape=out_shape,
        compiler_params=pltpu.CompilerParams(
            dimension_semantics=("arbitrary",), vmem_limit_bytes=VMEM_LIMIT),
        name="post_mlp",
    )(*x_parts, *a_parts, mod_l, g, w_proj, w_up, w_down, final_g)


def kernel(x_prompt, x_sample, cache_k, cache_v, c, c_ctx, norm_g, w_ada, b_ada, w_qkv, w_o, rpb,
           w_pw1, w_dw, b_dw, conv_ln_g, conv_ln_b, w_pw2, w_up, w_down, final_g):
    batch, seq, d = x_prompt.shape
    dec_batch, dec_seq, _ = x_sample.shape
    depth = w_ada.shape[0]
    n_attn = w_qkv.shape[0]
    past = cache_k.shape[2]
    n_prompt = batch * seq
    n_tok = n_prompt + dec_batch * dec_seq
    head_dim = d // N_HEADS
    assert d == N_HEADS * head_dim and 2 * head_dim == LANE and LANE == 2 * GRID_W
    assert seq == ATT_BLOCK and past == ATT_BLOCK and dec_seq % ATT_BLOCK == 0
    assert rpb.shape[1:] == (N_HEADS, N_DROW, N_DCOL)
    assert 1 + dec_batch <= N_COND_ROWS
    tm = 512
    dims = dict(n_prompt=n_prompt, n_tok=n_tok, latent_seq=dec_seq)

    x_parts = [x_prompt.reshape(n_prompt, d), x_sample.reshape(dec_batch * dec_seq, d)]
    cond = jnp.concatenate([c_ctx[None, :], c, jnp.zeros((N_COND_ROWS - 1 - dec_batch, d), F32)], axis=0)
    mod = _adaln(cond, w_ada, b_ada).reshape(depth, N_COND_ROWS, 6, d)
    ckt = jnp.transpose(cache_k, (0, 1, 3, 4, 2)).reshape(dec_batch, n_attn, d, past)
    cvt = jnp.transpose(cache_v, (0, 1, 3, 4, 2)).reshape(dec_batch, n_attn, d, past)
    final_g2 = final_g.reshape(1, d)
    w_qkv_b, w_o_b, w_pw1_b, w_pw2_b = [w.astype(BF16) for w in (w_qkv, w_o, w_pw1, w_pw2)]
    mlp_weights = (w_up, w_down)

    new_caches = ()
    for l in range(depth):
        i = l // 2
        mod_l = mod[l]
        g1 = norm_g[l, 0].reshape(1, d)
        g2 = norm_g[l, 1].reshape(1, d)
        if l % 2 == 0:
            q, kt, vt, ck_new, cv_new, ktp, vtp = _qkv(x_parts, mod_l, g1, w_qkv_b, new_caches, layer_i=i, seq=seq,
                                                       tm=tm, scale=head_dim ** -0.5, **dims)
            new_caches = (ck_new, cv_new)
            o_p = _prompt_attn(q, ktp, vtp, n_prompt=n_prompt, seq=seq)
            o_s, w_up_l, w_down_l = _latent_attn(rpb[i].reshape(-1), q, kt, vt, ckt, cvt, mlp_weights, layer_i=i,
                                                 layer=l, n_prompt=n_prompt, latent_seq=dec_seq)
            a_parts = [o_p, o_s]
            w_proj = w_o_b
        else:
            assert len(x_parts) == 1
            a, w_up_l, w_down_l = _conv_front(x_parts[0], mod_l, g1, w_pw1_b, w_dw[i], b_dw[i].reshape(1, d),
                                              conv_ln_g[i].reshape(1, d), conv_ln_b[i].reshape(1, d), mlp_weights,
                                              layer_i=i, layer=l, n_prompt=n_prompt, prompt_seq=seq,
                                              latent_seq=dec_seq)
            a_parts = [a]
            w_proj = w_pw2_b
        x_parts = _post(x_parts, a_parts, mod_l, g2, w_proj, w_up_l, w_down_l, final_g2, proj_i=i, tm=tm,
                        final=(l == depth - 1), **dims)

    y_prompt, y_sample = x_parts
    new_k, new_v = [jnp.transpose(t.reshape(batch, n_attn, N_HEADS, head_dim, seq), (0, 1, 4, 2, 3))
                    for t in new_caches]
    return (y_prompt.reshape(batch, seq, d), y_sample.reshape(dec_batch, dec_seq, d), new_k, new_v)
```

```python
import functools

import numpy as np
import jax
import jax.numpy as jnp
from jax import lax
from jax.experimental import pallas as pl
from jax.experimental.pallas import tpu as pltpu

F32 = jnp.float32
BF16 = jnp.bfloat16

RMS_EPS = 1e-6
LN_EPS = 1e-5
N_HEADS = 16
GRID_W = 64
WIN_ROWS = 8
WIN_COLS = 16
N_DROW = 2 * WIN_ROWS - 1
N_DCOL = 2 * WIN_COLS - 1
CONV_WIDTH = 31
CONV_PAD = 16
N_COND_ROWS = 8

LANE = 128
SUBLANE = 8
ATT_BLOCK = 256
ROWS_PER_BLOCK = ATT_BLOCK // GRID_W
VMEM_LIMIT = 56 * 1024 * 1024


def _rms_mod(xf, g, shift, scale):
    y = xf * lax.rsqrt(jnp.mean(xf * xf, axis=-1, keepdims=True) + RMS_EPS)
    return (y * g) * (1.0 + scale) + shift


def _cond_row(i, n_prompt_tiles, tiles_per_latent_seq):
    return jnp.where(i < n_prompt_tiles, 0, 1 + (i - n_prompt_tiles) // tiles_per_latent_seq)


def _const_spec(shape):
    nd = len(shape)
    return pl.BlockSpec(shape, lambda *_: (0,) * nd, pipeline_mode=pl.Buffered(1))


def _layer_spec(shape, layer):
    nd = len(shape)
    return pl.BlockSpec((None,) + tuple(shape), lambda *_: (layer,) + (0,) * nd, pipeline_mode=pl.Buffered(1))


def _token_specs(parts, tm, n_prompt_tiles):
    d = parts[0].shape[1]
    if len(parts) == 1:
        return [pl.BlockSpec((tm, d), lambda i: (i, 0))]
    return [pl.BlockSpec((tm, d), lambda i: (jnp.minimum(i, n_prompt_tiles - 1), 0)),
            pl.BlockSpec((tm, d), lambda i: (jnp.maximum(i - n_prompt_tiles, 0), 0))]


def _token_tile(refs, is_prompt):
    if len(refs) == 1:
        return refs[0][...]
    return jnp.where(is_prompt, refs[0][...], refs[1][...])


def _cast_specs(weights, layer, n_chunks, step_of):
    in_specs, out_specs, out_shapes = [], [], []
    for w in weights:
        _, r, c = w.shape
        chunk = lambda *idx: jnp.minimum(step_of(*idx), n_chunks - 1)
        in_specs.append(pl.BlockSpec((None, r // n_chunks, c), lambda *idx, chunk=chunk: (layer, chunk(*idx), 0)))
        out_specs.append(pl.BlockSpec((r // n_chunks, c), lambda *idx, chunk=chunk: (chunk(*idx), 0)))
        out_shapes.append(jax.ShapeDtypeStruct((r, c), BF16))
    return in_specs, out_specs, out_shapes


def _cast_chunks(step, n_chunks, w_refs, o_refs):
    @pl.when(step < n_chunks)
    def _():
        for w_ref, o_ref in zip(w_refs, o_refs):
            o_ref[...] = w_ref[...].astype(BF16)


def _adaln_kernel(cond_ref, w_ref, b_ref, o_ref):
    s = cond_ref[...]
    s = s * jax.nn.sigmoid(s)
    o_ref[0] = jnp.dot(s.astype(BF16), w_ref[0].astype(BF16), preferred_element_type=F32) + b_ref[0]


def _adaln(cond, w_ada, b_ada):
    depth, d, n6 = w_ada.shape
    tn = n6 // 4
    return pl.pallas_call(
        _adaln_kernel,
        grid=(depth, n6 // tn),
        in_specs=[
            pl.BlockSpec((N_COND_ROWS, d), lambda l, j: (0, 0)),
            pl.BlockSpec((1, d, tn), lambda l, j: (l, 0, j)),
            pl.BlockSpec((1, 1, tn), lambda l, j: (l, 0, j)),
        ],
        out_specs=pl.BlockSpec((1, N_COND_ROWS, tn), lambda l, j: (l, 0, j)),
        out_shape=jax.ShapeDtypeStruct((depth, N_COND_ROWS, n6), F32),
        compiler_params=pltpu.CompilerParams(
            dimension_semantics=("arbitrary", "arbitrary"), vmem_limit_bytes=VMEM_LIMIT),
        name="adaln",
    )(cond, w_ada, b_ada.reshape(depth, 1, n6))


def _qkv_kernel(*refs, n_x, n_prev, layer_i, n_prompt_tiles, scale):
    x_refs = refs[:n_x]
    mod_ref, g_ref, w_ref = refs[n_x:n_x + 3]
    q_ref, ktl_ref, vtl_ref, kt_ref, vt_ref, ktp_ref, vtp_ref, wt_ref = refs[n_x + 3 + n_prev:]
    d = q_ref.shape[1]
    n_seq, seq = kt_ref.shape[0], kt_ref.shape[-1]

    @pl.when(pl.program_id(0) == 0)
    def _():
        for r in range(0, 2 * d, ATT_BLOCK):
            for c in range(0, d, ATT_BLOCK):
                wt_ref[r:r + ATT_BLOCK, c:c + ATT_BLOCK] = w_ref[c:c + ATT_BLOCK, d + r:d + r + ATT_BLOCK].T

    is_prompt = pl.program_id(0) < n_prompt_tiles
    h = _rms_mod(_token_tile(x_refs, is_prompt), g_ref[...], mod_ref[0:1, :], mod_ref[1:2, :]).astype(BF16)
    q_ref[...] = (jnp.dot(h, w_ref[:, :d], preferred_element_type=F32) * scale).astype(BF16)

    @pl.when(is_prompt)
    def _():
        for s in range(n_seq):
            hs = h[s * seq:(s + 1) * seq, :]
            for t_ref, tb_ref, lo in ((kt_ref, ktp_ref, 0), (vt_ref, vtp_ref, d)):
                t = _nt_dot(wt_ref[lo:lo + d, :], hs)
                tb_ref[s] = t.astype(BF16)
                if n_prev:
                    t_ref[s] = t
                else:
                    for j in range(t_ref.shape[1]):
                        t_ref[s, j] = t if j == layer_i else jnp.zeros_like(t)

    @pl.when(jnp.logical_not(is_prompt))
    def _():
        ktl_ref[...] = _nt_dot(wt_ref[:d, :], h).astype(BF16)
        vtl_ref[...] = _nt_dot(wt_ref[d:, :], h).astype(BF16)


def _qkv(x_parts, mod_l, g, w_qkv, prev_caches, *, layer_i, seq, n_prompt, n_tok, latent_seq, tm, scale):
    d = x_parts[0].shape[1]
    n_attn = w_qkv.shape[0]
    npt = n_prompt // tm
    tiles_per_seq = latent_seq // tm
    row = functools.partial(_cond_row, n_prompt_tiles=npt, tiles_per_latent_seq=tiles_per_seq)
    tok = pl.BlockSpec((tm, d), lambda i: (i, 0))
    latent_t = pl.BlockSpec((None, d, tm), lambda i: (jnp.maximum(i - npt, 0) // tiles_per_seq, 0,
                                                     jnp.maximum(i - npt, 0) % tiles_per_seq))
    if prev_caches:
        cache_blk = pl.BlockSpec((tm // seq, None, d, seq), lambda i: (jnp.minimum(i, npt - 1), layer_i, 0, 0))
    else:
        cache_blk = pl.BlockSpec((tm // seq, n_attn, d, seq), lambda i: (jnp.minimum(i, npt - 1), 0, 0, 0))
    cache_shape = jax.ShapeDtypeStruct((n_prompt // seq, n_attn, d, seq), F32)
    latent_shape = jax.ShapeDtypeStruct(((n_tok - n_prompt) // latent_seq, d, latent_seq), BF16)
    prompt_t = pl.BlockSpec((tm // seq, d, seq), lambda i: (jnp.minimum(i, npt - 1), 0, 0))
    prompt_shape = jax.ShapeDtypeStruct((n_prompt // seq, d, seq), BF16)
    n_in = len(x_parts) + 3
    return pl.pallas_call(
        functools.partial(_qkv_kernel, n_x=len(x_parts), n_prev=len(prev_caches), layer_i=layer_i,
                          n_prompt_tiles=npt, scale=scale),
        grid=(n_tok // tm,),
        in_specs=_token_specs(x_parts, tm, npt) + [
            pl.BlockSpec((None, 6, d), lambda i: (row(i), 0, 0)),
            _const_spec((1, d)),
            _layer_spec((d, 3 * d), layer_i),
        ] + [pl.BlockSpec(memory_space=pl.ANY)] * len(prev_caches),
        out_specs=[tok, latent_t, latent_t, cache_blk, cache_blk, prompt_t, prompt_t],
        out_shape=[jax.ShapeDtypeStruct((n_tok, d), BF16), latent_shape, latent_shape, cache_shape, cache_shape,
                   prompt_shape, prompt_shape],
        input_output_aliases={n_in + j: 3 + j for j in range(len(prev_caches))},
        scratch_shapes=[pltpu.VMEM((2 * d, d), BF16)],
        compiler_params=pltpu.CompilerParams(
            dimension_semantics=("arbitrary",), vmem_limit_bytes=VMEM_LIMIT),
        name="qkv",
    )(*x_parts, mod_l, g, w_qkv, *prev_caches)


def _nt_dot(a, b):
    return lax.dot_general(a, b, (((1,), (1,)), ((), ())), preferred_element_type=F32)


def _pair_heads(qp, first):
    zero = jnp.zeros_like(qp)
    return jnp.where(first, qp, zero), jnp.where(first, zero, qp)


def _prompt_attn_kernel(q_ref, kt_ref, vt_ref, o_ref):
    n_seq, d, seq = kt_ref.shape
    half = LANE // 2
    first = lax.broadcasted_iota(jnp.int32, (seq, LANE), 1) < half
    ones = jnp.ones((2 * SUBLANE, seq), BF16)

    def scores(s, p):
        sl = slice(p * LANE, (p + 1) * LANE)
        ktp = kt_ref[s, sl, :]
        return [jnp.dot(qm, ktp, preferred_element_type=F32)
                for qm in _pair_heads(q_ref[s * seq:(s + 1) * seq, sl], first)]

    def weights(pair_scores):
        return [jnp.exp(sc - jnp.max(sc, axis=-1, keepdims=True)).astype(BF16) for sc in pair_scores]

    def finish(s, p, pair_weights):
        sl = slice(p * LANE, (p + 1) * LANE)
        vtp = jnp.concatenate([vt_ref[s, sl, :], ones], axis=0)
        outs = []
        for e in pair_weights:
            ot = _nt_dot(vtp, e)
            outs.append(ot[:LANE, :] / ot[LANE:LANE + 1, :])
        ot = jnp.concatenate([outs[0][:half, :], outs[1][half:, :]], axis=0)
        o_ref[s * seq:(s + 1) * seq, sl] = ot.T.astype(BF16)

    chains = [(s, p) for s in range(n_seq) for p in range(d // LANE)]
    s_next, e_prev = scores(*chains[0]), None
    for n in range(len(chains) + 1):
        s_cur, s_next = s_next, (scores(*chains[n + 1]) if n + 1 < len(chains) else None)
        e_cur = weights(s_cur) if n < len(chains) else None
        if e_prev is not None:
            finish(*chains[n - 1], e_prev)
        e_prev = e_cur


def _prompt_attn(q, kt, vt, *, n_prompt, seq):
    d = q.shape[1]
    n_seq = 4
    tok = pl.BlockSpec((n_seq * seq, d), lambda b: (b, 0))
    feat = pl.BlockSpec((n_seq, d, seq), lambda b: (b, 0, 0))
    return pl.pallas_call(
        _prompt_attn_kernel,
        grid=(n_prompt // (n_seq * seq),),
        in_specs=[tok, feat, feat],
        out_specs=tok,
        out_shape=jax.ShapeDtypeStruct((n_prompt, d), BF16),
        compiler_params=pltpu.CompilerParams(
            dimension_semantics=("arbitrary",), vmem_limit_bytes=VMEM_LIMIT),
        name="prompt_attn",
    )(q, kt, vt)


def _row_start(r, rows):
    kr = min(WIN_ROWS, rows)
    return min(max(r - kr // 2, 0), rows - kr)


def _latent_block_plan(rows):
    kr = min(WIN_ROWS, rows)
    plan = []
    for qb in range(rows // ROWS_PER_BLOCK):
        starts = [_row_start(r, rows) for r in range(qb * ROWS_PER_BLOCK, (qb + 1) * ROWS_PER_BLOCK)]
        plan.append((min(starts) // ROWS_PER_BLOCK, (max(starts) + kr - 1) // ROWS_PER_BLOCK + 1))
    return plan


def _build_pair_bias(rpb_ref, tile_ref, bias_ref, head0, *, plan, rows):
    kr = min(WIN_ROWS, rows)
    shape = (GRID_W, LANE)
    qc = lax.broadcasted_iota(jnp.int32, shape, 0)
    lane = lax.broadcasted_iota(jnp.int32, shape, 1)
    kc = lane & (GRID_W - 1)
    delta = kc - qc + (WIN_COLS - 1)
    col_start = jnp.clip(qc - WIN_COLS // 2, 0, GRID_W - WIN_COLS)
    in_window = (kc >= col_start) & (kc < col_start + WIN_COLS)
    left = lane < GRID_W
    neg = jnp.full(shape, -jnp.inf, F32)
    for hh in range(2):
        base = (head0 + hh) * (N_DROW * N_DCOL)
        for dr in range(N_DROW):
            w = neg
            for dc in range(N_DCOL):
                w = jnp.where(delta == dc, rpb_ref[base + dr * N_DCOL + dc], w)
            tile_ref[dr] = jnp.where(in_window, w, neg)
        t = 0
        for qb, (lo, hi) in enumerate(plan):
            for kb in range(lo, hi):
                for a in range(ROWS_PER_BLOCK):
                    r = qb * ROWS_PER_BLOCK + a
                    start = _row_start(r, rows)
                    for c in range(ATT_BLOCK // LANE):
                        halves = []
                        for jj in range(LANE // GRID_W):
                            k_row = kb * ROWS_PER_BLOCK + c * (LANE // GRID_W) + jj
                            ok = start <= k_row < start + kr
                            halves.append(tile_ref[k_row - r + WIN_ROWS - 1] if ok else None)
                        if halves[0] is None and halves[1] is None:
                            val = neg
                        else:
                            val = jnp.where(left, neg if halves[0] is None else halves[0],
                                            neg if halves[1] is None else halves[1])
                        bias_ref[hh, t, a * GRID_W:(a + 1) * GRID_W, c * LANE:(c + 1) * LANE] = val
                t += 1


def _latent_attn_kernel(rpb_ref, q_ref, kt_ref, vt_ref, kct_ref, vct_ref, wup_ref, wdn_ref, o_ref, wup_o_ref,
                        wdn_o_ref, tile_ref, bias_ref, *, plan, rows, n_batch, n_chunks):
    _cast_chunks(pl.program_id(0) * n_batch + pl.program_id(1), n_chunks, (wup_ref, wdn_ref),
                 (wup_o_ref, wdn_o_ref))

    @pl.when(pl.program_id(1) == 0)
    def _():
        _build_pair_bias(rpb_ref, tile_ref, bias_ref, 2 * pl.program_id(0), plan=plan, rows=rows)

    half = LANE // 2
    first = lax.broadcasted_iota(jnp.int32, (ATT_BLOCK, LANE), 1) < half
    kct = kct_ref[...].astype(BF16)
    vct = jnp.concatenate([vct_ref[...].astype(BF16), jnp.ones((2 * SUBLANE, kct.shape[1]), BF16)], axis=0)
    vt = jnp.concatenate([vt_ref[...], jnp.ones((2 * SUBLANE, vt_ref.shape[1]), BF16)], axis=0)
    bias_starts = [sum(hi - lo for lo, hi in plan[:qb]) for qb in range(len(plan))]

    def scores(qb):
        lo, hi = plan[qb]
        keys = slice(lo * ATT_BLOCK, hi * ATT_BLOCK)
        out = []
        for hh, qm in enumerate(_pair_heads(q_ref[qb * ATT_BLOCK:(qb + 1) * ATT_BLOCK, :], first)):
            bias = jnp.concatenate([bias_ref[hh, bias_starts[qb] + j] for j in range(hi - lo)], axis=1)
            out.append((jnp.dot(qm, kt_ref[:, keys], preferred_element_type=F32) + bias,
                        jnp.dot(qm, kct, preferred_element_type=F32)))
        return out

    def weights(block_scores):
        out = []
        for s_loc, s_ctx in block_scores:
            m = jnp.maximum(jnp.max(s_loc, axis=-1, keepdims=True), jnp.max(s_ctx, axis=-1, keepdims=True))
            out.append((jnp.exp(s_loc - m).astype(BF16), jnp.exp(s_ctx - m).astype(BF16)))
        return out

    def finish(qb, block_weights):
        lo, hi = plan[qb]
        keys = slice(lo * ATT_BLOCK, hi * ATT_BLOCK)
        outs = []
        for e_loc, e_ctx in block_weights:
            ot = _nt_dot(vt[:, keys], e_loc) + _nt_dot(vct, e_ctx)
            outs.append(ot[:LANE, :] / ot[LANE:LANE + 1, :])
        ot = jnp.concatenate([outs[0][:half, :], outs[1][half:, :]], axis=0)
        o_ref[qb * ATT_BLOCK:(qb + 1) * ATT_BLOCK, :] = ot.T.astype(BF16)

    pending = scores(0)
    for qb in range(len(plan)):
        upcoming = scores(qb + 1) if qb + 1 < len(plan) else None
        finish(qb, weights(pending))
        pending = upcoming


def _latent_attn(rpb_flat, q, kt, vt, cache_kt, cache_vt, mlp_weights, *, layer_i, layer, n_prompt, latent_seq):
    d = q.shape[1]
    n_batch, _, _, past = cache_kt.shape
    rows = latent_seq // GRID_W
    plan = _latent_block_plan(rows)
    n_bias = sum(hi - lo for lo, hi in plan)
    first_latent = n_prompt // latent_seq
    n_steps = (d // LANE) * n_batch
    tok_q = pl.BlockSpec((latent_seq, LANE), lambda p, b: (first_latent + b, p))
    feat = pl.BlockSpec((None, LANE, latent_seq), lambda p, b: (b, p, 0))
    ctx = pl.BlockSpec((None, None, LANE, past), lambda p, b: (b, layer_i, p, 0))
    cast_in, cast_out, cast_shapes = _cast_specs(mlp_weights, layer, n_steps, lambda p, b: p * n_batch + b)
    return pl.pallas_call(
        functools.partial(_latent_attn_kernel, plan=plan, rows=rows, n_batch=n_batch, n_chunks=n_steps),
        grid=(d // LANE, n_batch),
        in_specs=[pl.BlockSpec(memory_space=pltpu.SMEM), tok_q, feat, feat, ctx, ctx] + cast_in,
        out_specs=[pl.BlockSpec((latent_seq, LANE), lambda p, b: (b, p))] + cast_out,
        out_shape=[jax.ShapeDtypeStruct((n_batch * latent_seq, d), BF16)] + cast_shapes,
        scratch_shapes=[pltpu.VMEM((N_DROW, GRID_W, LANE), F32),
                        pltpu.VMEM((2, n_bias, ATT_BLOCK, ATT_BLOCK), F32)],
        compiler_params=pltpu.CompilerParams(
            dimension_semantics=("arbitrary", "arbitrary"), vmem_limit_bytes=VMEM_LIMIT),
        name="latent_attn",
    )(rpb_flat, q, kt, vt, cache_kt, cache_vt, *mlp_weights)


def _conv_kernel(x_ref, mod_ref, g_ref, w1_ref, wdw_ref, bdw_ref, lng_ref, lnb_ref, wup_ref, wdn_ref, a_ref,
                 wup_o_ref, wdn_o_ref, pad_ref, *, n_prompt_tiles, prompt_seq, rows_per_step, n_chunks):
    _cast_chunks(pl.program_id(0), n_chunks, (wup_ref, wdn_ref), (wup_o_ref, wdn_o_ref))
    tm, d = x_ref.shape
    h = _rms_mod(x_ref[...], g_ref[...], mod_ref[0:1, :], mod_ref[1:2, :])
    ag = jnp.dot(h.astype(BF16), w1_ref[...], preferred_element_type=F32)
    u = ag[:, :d] * jax.nn.sigmoid(ag[:, d:])
    first_tap = CONV_PAD - CONV_WIDTH // 2
    rs = rows_per_step

    def conv(seq):
        stride = seq + 2 * CONV_PAD
        for s in range(tm // seq):
            base = s * stride
            pad_ref[base:base + CONV_PAD, :] = jnp.zeros((CONV_PAD, d), F32)
            pad_ref[base + CONV_PAD:base + CONV_PAD + seq, :] = u[s * seq:(s + 1) * seq, :]
            pad_ref[base + CONV_PAD + seq:base + stride, :] = jnp.zeros((CONV_PAD, d), F32)
        for s in range(tm // seq):
            base = s * stride

            def step(ci, carry, base=base, s=s):
                r0 = pl.multiple_of(ci * rs, rs)
                parts = []
                for lc in range(d // LANE):
                    ls = slice(lc * LANE, (lc + 1) * LANE)
                    win = pad_ref[pl.ds(base + r0, rs + 2 * CONV_PAD), ls]
                    acc = jnp.broadcast_to(bdw_ref[:, ls], (rs, LANE))
                    for b in range(SUBLANE):
                        vb = None
                        for o in range(b, CONV_WIDTH + first_tap, SUBLANE):
                            t = o - first_tap
                            if t < 0:
                                continue
                            term = win[o - b:o - b + rs + SUBLANE, :] * wdw_ref[t:t + 1, ls]
                            vb = term if vb is None else vb + term
                        if b:
                            vb = pltpu.roll(vb, rs + SUBLANE - b, axis=0)
                        acc = acc + vb[:rs, :]
                    parts.append(acc)
                acc = jnp.concatenate(parts, axis=1)
                mu = jnp.mean(acc, axis=-1, keepdims=True)
                xc = acc - mu
                y = xc * lax.rsqrt(jnp.mean(xc * xc, axis=-1, keepdims=True) + LN_EPS)
                y = y * lng_ref[...] + lnb_ref[...]
                a_ref[pl.ds(s * seq + r0, rs), :] = (y * jax.nn.sigmoid(y)).astype(BF16)
                return carry

            lax.fori_loop(0, seq // rs, step, 0)

    is_prompt = pl.program_id(0) < n_prompt_tiles
    pl.when(is_prompt)(lambda: conv(prompt_seq))
    pl.when(jnp.logical_not(is_prompt))(lambda: conv(tm))


def _conv_front(x, mod_l, g, w_pw1, w_dw, b_dw, ln_g, ln_b, mlp_weights, *, layer_i, layer, n_prompt, prompt_seq,
                latent_seq):
    n_tok, d = x.shape
    tm = latent_seq
    npt = n_prompt // tm
    n_chunks = 8
    assert n_tok // tm >= n_chunks
    row = functools.partial(_cond_row, n_prompt_tiles=npt, tiles_per_latent_seq=1)
    tok = pl.BlockSpec((tm, d), lambda i: (i, 0))
    pad_rows = (tm // prompt_seq) * (prompt_seq + 2 * CONV_PAD)
    cast_in, cast_out, cast_shapes = _cast_specs(mlp_weights, layer, n_chunks, lambda i: i)
    return pl.pallas_call(
        functools.partial(_conv_kernel, n_prompt_tiles=npt, prompt_seq=prompt_seq, rows_per_step=256,
                          n_chunks=n_chunks),
        grid=(n_tok // tm,),
        in_specs=[
            tok,
            pl.BlockSpec((None, 6, d), lambda i: (row(i), 0, 0)),
            _const_spec((1, d)),
            _layer_spec((d, 2 * d), layer_i),
            _const_spec((CONV_WIDTH, d)),
            _const_spec((1, d)),
            _const_spec((1, d)),
            _const_spec((1, d)),
        ] + cast_in,
        out_specs=[tok] + cast_out,
        out_shape=[jax.ShapeDtypeStruct((n_tok, d), BF16)] + cast_shapes,
        scratch_shapes=[pltpu.VMEM((pad_rows, d), F32)],
        compiler_params=pltpu.CompilerParams(
            dimension_semantics=("arbitrary",), vmem_limit_bytes=VMEM_LIMIT),
        name="conv_front",
    )(x, mod_l, g, w_pw1, w_dw, b_dw, ln_g, ln_b, *mlp_weights)


def _post_kernel(*refs, n_x, n_a, n_out, n_prompt_tiles, ff_chunk, final):
    x_refs, a_refs = refs[:n_x], refs[n_x:n_x + n_a]
    mod_ref, g_ref, wp_ref, wup_ref, wdn_ref, fg_ref = refs[n_x + n_a:n_x + n_a + 6]
    o_refs = refs[n_x + n_a + 6:]
    is_prompt = pl.program_id(0) < n_prompt_tiles
    y = jnp.dot(_token_tile(a_refs, is_prompt), wp_ref[...], preferred_element_type=F32)
    x1 = _token_tile(x_refs, is_prompt) + mod_ref[2:3, :] * y
    h = _rms_mod(x1, g_ref[...], mod_ref[3:4, :], mod_ref[4:5, :]).astype(BF16)
    acc = jnp.zeros_like(x1)
    for c in range(wup_ref.shape[1] // ff_chunk):
        sl = slice(c * ff_chunk, (c + 1) * ff_chunk)
        u = jnp.maximum(jnp.dot(h, wup_ref[:, sl], preferred_element_type=F32), 0.0)
        acc = acc + jnp.dot((u * u).astype(BF16), wdn_ref[sl, :], preferred_element_type=F32)
    out = x1 + mod_ref[5:6, :] * acc
    if final:
        out = out * lax.rsqrt(jnp.mean(out * out, axis=-1, keepdims=True) + RMS_EPS) * fg_ref[...]
    if n_out == 1:
        o_refs[0][...] = out
    else:
        @pl.when(is_prompt)
        def _():
            o_refs[0][...] = out

        @pl.when(jnp.logical_not(is_prompt))
        def _():
            o_refs[1][...] = out


def _post(x_parts, a_parts, mod_l, g, w_proj, w_up, w_down, final_g, *, proj_i, n_prompt, n_tok, latent_seq, tm,
          final):
    d = x_parts[0].shape[1]
    d_ff = w_up.shape[1]
    npt = n_prompt // tm
    row = functools.partial(_cond_row, n_prompt_tiles=npt, tiles_per_latent_seq=latent_seq // tm)
    if final:
        out_shape = [jax.ShapeDtypeStruct((n_prompt, d), F32), jax.ShapeDtypeStruct((n_tok - n_prompt, d), F32)]
    else:
        out_shape = [jax.ShapeDtypeStruct((n_tok, d), F32)]
    return pl.pallas_call(
        functools.partial(_post_kernel, n_x=len(x_parts), n_a=len(a_parts), n_out=len(out_shape),
                          n_prompt_tiles=npt, ff_chunk=1024, final=final),
        grid=(n_tok // tm,),
        in_specs=_token_specs(x_parts, tm, npt) + _token_specs(a_parts, tm, npt) + [
            pl.BlockSpec((None, 6, d), lambda i: (row(i), 0, 0)),
            _const_spec((1, d)),
            _layer_spec((d, d), proj_i),
            _const_spec((d, d_ff)),
            _const_spec((d_ff, d)),
            _const_spec((1, d)),
        ],
        out_specs=_token_specs(out_shape, tm, npt),
        out_shape=out_shape,
        compiler_params=pltpu.CompilerParams(
            dimension_semantics=("arbitrary",), vmem_limit_bytes=VMEM_LIMIT),
        name="post_mlp",
    )(*x_parts, *a_parts, mod_l, g, w_proj, w_up, w_down, final_g)


def kernel(x_prompt, x_sample, cache_k, cache_v, c, c_ctx, norm_g, w_ada, b_ada, w_qkv, w_o, rpb,
           w_pw1, w_dw, b_dw, conv_ln_g, conv_ln_b, w_pw2, w_up, w_down, final_g):
    batch, seq, d = x_prompt.shape
    dec_batch, dec_seq, _ = x_sample.shape
    depth = w_ada.shape[0]
    n_attn = w_qkv.shape[0]
    past = cache_k.shape[2]
    n_prompt = batch * seq
    n_tok = n_prompt + dec_batch * dec_seq
    head_dim = d // N_HEADS
    assert d == N_HEADS * head_dim and 2 * head_dim == LANE and LANE == 2 * GRID_W
    assert seq == ATT_BLOCK and past == ATT_BLOCK and dec_seq % ATT_BLOCK == 0
    assert rpb.shape[1:] == (N_HEADS, N_DROW, N_DCOL)
    assert 1 + dec_batch <= N_COND_ROWS
    tm = 512
    dims = dict(n_prompt=n_prompt, n_tok=n_tok, latent_seq=dec_seq)

    x_parts = [x_prompt.reshape(n_prompt, d), x_sample.reshape(dec_batch * dec_seq, d)]
    cond = jnp.concatenate([c_ctx[None, :], c, jnp.zeros((N_COND_ROWS - 1 - dec_batch, d), F32)], axis=0)
    mod = _adaln(cond, w_ada, b_ada).reshape(depth, N_COND_ROWS, 6, d)
    ckt = jnp.transpose(cache_k, (0, 1, 3, 4, 2)).reshape(dec_batch, n_attn, d, past)
    cvt = jnp.transpose(cache_v, (0, 1, 3, 4, 2)).reshape(dec_batch, n_attn, d, past)
    final_g2 = final_g.reshape(1, d)
    w_qkv_b, w_o_b, w_pw1_b, w_pw2_b = [w.astype(BF16) for w in (w_qkv, w_o, w_pw1, w_pw2)]
    mlp_weights = (w_up, w_down)

    new_caches = ()
    for l in range(depth):
        i = l // 2
        mod_l = mod[l]
        g1 = norm_g[l, 0].reshape(1, d)
        g2 = norm_g[l, 1].reshape(1, d)
        if l % 2 == 0:
            q, kt, vt, ck_new, cv_new, ktp, vtp = _qkv(x_parts, mod_l, g1, w_qkv_b, new_caches, layer_i=i, seq=seq,
                                                       tm=tm, scale=head_dim ** -0.5, **dims)
            new_caches = (ck_new, cv_new)
            o_p = _prompt_attn(q, ktp, vtp, n_prompt=n_prompt, seq=seq)
            o_s, w_up_l, w_down_l = _latent_attn(rpb[i].reshape(-1), q, kt, vt, ckt, cvt, mlp_weights, layer_i=i,
                                                 layer=l, n_prompt=n_prompt, latent_seq=dec_seq)
            a_parts = [o_p, o_s]
            w_proj = w_o_b
        else:
            assert len(x_parts) == 1
            a, w_up_l, w_down_l = _conv_front(x_parts[0], mod_l, g1, w_pw1_b, w_dw[i], b_dw[i].reshape(1, d),
                                              conv_ln_g[i].reshape(1, d), conv_ln_b[i].reshape(1, d), mlp_weights,
                                              layer_i=i, layer=l, n_prompt=n_prompt, prompt_seq=seq,
                                              latent_seq=dec_seq)
            a_parts = [a]
            w_proj = w_pw2_b
        x_parts = _post(x_parts, a_parts, mod_l, g2, w_proj, w_up_l, w_down_l, final_g2, proj_i=i, tm=tm,
                        final=(l == depth - 1), **dims)

    y_prompt, y_sample = x_parts
    new_k, new_v = [jnp.transpose(t.reshape(batch, n_attn, N_HEADS, head_dim, seq), (0, 1, 4, 2, 3))
                    for t in new_caches]
    return (y_prompt.reshape(batch, seq, d), y_sample.reshape(dec_batch, dec_seq, d), new_k, new_v)
```

```python
import functools

import numpy as np
import jax
import jax.numpy as jnp
from jax import lax
from jax.experimental import pallas as pl
from jax.experimental.pallas import tpu as pltpu

F32 = jnp.float32
BF16 = jnp.bfloat16

RMS_EPS = 1e-6
LN_EPS = 1e-5
N_HEADS = 16
GRID_W = 64
WIN_ROWS = 8
WIN_COLS = 16
N_DROW = 2 * WIN_ROWS - 1
N_DCOL = 2 * WIN_COLS - 1
CONV_WIDTH = 31
CONV_PAD = 16
N_COND_ROWS = 8

LANE = 128
SUBLANE = 8
ATT_BLOCK = 256
ROWS_PER_BLOCK = ATT_BLOCK // GRID_W
VMEM_LIMIT = 56 * 1024 * 1024


def _rms_mod(xf, g, shift, scale):
    y = xf * lax.rsqrt(jnp.mean(xf * xf, axis=-1, keepdims=True) + RMS_EPS)
    return (y * g) * (1.0 + scale) + shift


def _cond_row(i, n_prompt_tiles, tiles_per_latent_seq):
    return jnp.where(i < n_prompt_tiles, 0, 1 + (i - n_prompt_tiles) // tiles_per_latent_seq)


def _const_spec(shape):
    nd = len(shape)
    return pl.BlockSpec(shape, lambda *_: (0,) * nd, pipeline_mode=pl.Buffered(1))


def _layer_spec(shape, layer):
    nd = len(shape)
    return pl.BlockSpec((None,) + tuple(shape), lambda *_: (layer,) + (0,) * nd, pipeline_mode=pl.Buffered(1))


def _token_specs(parts, tm, n_prompt_tiles):
    d = parts[0].shape[1]
    if len(parts) == 1:
        return [pl.BlockSpec((tm, d), lambda i: (i, 0))]
    return [pl.BlockSpec((tm, d), lambda i: (jnp.minimum(i, n_prompt_tiles - 1), 0)),
            pl.BlockSpec((tm, d), lambda i: (jnp.maximum(i - n_prompt_tiles, 0), 0))]


def _token_tile(refs, is_prompt):
    if len(refs) == 1:
        return refs[0][...]
    return jnp.where(is_prompt, refs[0][...], refs[1][...])


def _cast_specs(weights, layer, n_chunks, step_of):
    in_specs, out_specs, out_shapes = [], [], []
    for w in weights:
        _, r, c = w.shape
        chunk = lambda *idx: jnp.minimum(step_of(*idx), n_chunks - 1)
        in_specs.append(pl.BlockSpec((None, r // n_chunks, c), lambda *idx, chunk=chunk: (layer, chunk(*idx), 0)))
        out_specs.append(pl.BlockSpec((r // n_chunks, c), lambda *idx, chunk=chunk: (chunk(*idx), 0)))
        out_shapes.append(jax.ShapeDtypeStruct((r, c), BF16))
    return in_specs, out_specs, out_shapes


def _cast_chunks(step, n_chunks, w_refs, o_refs):
    @pl.when(step < n_chunks)
    def _():
        for w_ref, o_ref in zip(w_refs, o_refs):
            o_ref[...] = w_ref[...].astype(BF16)


def _adaln_kernel(cond_ref, w_ref, b_ref, o_ref):
    s = cond_ref[...]
    s = s * jax.nn.sigmoid(s)
    o_ref[0] = jnp.dot(s.astype(BF16), w_ref[0].astype(BF16), preferred_element_type=F32) + b_ref[0]


def _adaln(cond, w_ada, b_ada):
    depth, d, n6 = w_ada.shape
    tn = n6 // 4
    return pl.pallas_call(
        _adaln_kernel,
        grid=(depth, n6 // tn),
        in_specs=[
            pl.BlockSpec((N_COND_ROWS, d), lambda l, j: (0, 0)),
            pl.BlockSpec((1, d, tn), lambda l, j: (l, 0, j)),
            pl.BlockSpec((1, 1, tn), lambda l, j: (l, 0, j)),
        ],
        out_specs=pl.BlockSpec((1, N_COND_ROWS, tn), lambda l, j: (l, 0, j)),
        out_shape=jax.ShapeDtypeStruct((depth, N_COND_ROWS, n6), F32),
        compiler_params=pltpu.CompilerParams(
            dimension_semantics=("arbitrary", "arbitrary"), vmem_limit_bytes=VMEM_LIMIT),
        name="adaln",
    )(cond, w_ada, b_ada.reshape(depth, 1, n6))


def _qkv_kernel(*refs, n_x, n_prev, layer_i, n_prompt_tiles, scale):
    x_refs = refs[:n_x]
    mod_ref, g_ref, w_ref = refs[n_x:n_x + 3]
    q_ref, ktl_ref, vtl_ref, kt_ref, vt_ref, ktp_ref, vtp_ref, wt_ref = refs[n_x + 3 + n_prev:]
    d = q_ref.shape[1]
    n_seq, seq = kt_ref.shape[0], kt_ref.shape[-1]

    @pl.when(pl.program_id(0) == 0)
    def _():
        for r in range(0, 2 * d, ATT_BLOCK):
            for c in range(0, d, ATT_BLOCK):
                wt_ref[r:r + ATT_BLOCK, c:c + ATT_BLOCK] = w_ref[c:c + ATT_BLOCK, d + r:d + r + ATT_BLOCK].T

    is_prompt = pl.program_id(0) < n_prompt_tiles
    h = _rms_mod(_token_tile(x_refs, is_prompt), g_ref[...], mod_ref[0:1, :], mod_ref[1:2, :]).astype(BF16)
    q_ref[...] = (jnp.dot(h, w_ref[:, :d], preferred_element_type=F32) * scale).astype(BF16)

    @pl.when(is_prompt)
    def _():
        for s in range(n_seq):
            hs = h[s * seq:(s + 1) * seq, :]
            for t_ref, tb_ref, lo in ((kt_ref, ktp_ref, 0), (vt_ref, vtp_ref, d)):
                t = _nt_dot(wt_ref[lo:lo + d, :], hs)
                tb_ref[s] = t.astype(BF16)
                if n_prev:
                    t_ref[s] = t
                else:
                    for j in range(t_ref.shape[1]):
                        t_ref[s, j] = t if j == layer_i else jnp.zeros_like(t)

    @pl.when(jnp.logical_not(is_prompt))
    def _():
        ktl_ref[...] = _nt_dot(wt_ref[:d, :], h).astype(BF16)
        vtl_ref[...] = _nt_dot(wt_ref[d:, :], h).astype(BF16)


def _qkv(x_parts, mod_l, g, w_qkv, prev_caches, *, layer_i, seq, n_prompt, n_tok, latent_seq, tm, scale):
    d = x_parts[0].shape[1]
    n_attn = w_qkv.shape[0]
    npt = n_prompt // tm
    tiles_per_seq = latent_seq // tm
    row = functools.partial(_cond_row, n_prompt_tiles=npt, tiles_per_latent_seq=tiles_per_seq)
    tok = pl.BlockSpec((tm, d), lambda i: (i, 0))
    latent_t = pl.BlockSpec((None, d, tm), lambda i: (jnp.maximum(i - npt, 0) // tiles_per_seq, 0,
                                                     jnp.maximum(i - npt, 0) % tiles_per_seq))
    if prev_caches:
        cache_blk = pl.BlockSpec((tm // seq, None, d, seq), lambda i: (jnp.minimum(i, npt - 1), layer_i, 0, 0))
    else:
        cache_blk = pl.BlockSpec((tm // seq, n_attn, d, seq), lambda i: (jnp.minimum(i, npt - 1), 0, 0, 0))
    cache_shape = jax.ShapeDtypeStruct((n_prompt // seq, n_attn, d, seq), F32)
    latent_shape = jax.ShapeDtypeStruct(((n_tok - n_prompt) // latent_seq, d, latent_seq), BF16)
    prompt_t = pl.BlockSpec((tm // seq, d, seq), lambda i: (jnp.minimum(i, npt - 1), 0, 0))
    prompt_shape = jax.ShapeDtypeStruct((n_prompt // seq, d, seq), BF16)
    n_in = len(x_parts) + 3
    return pl.pallas_call(
        functools.partial(_qkv_kernel, n_x=len(x_parts), n_prev=len(prev_caches), layer_i=layer_i,
                          n_prompt_tiles=npt, scale=scale),
        grid=(n_tok // tm,),
        in_specs=_token_specs(x_parts, tm, npt) + [
            pl.BlockSpec((None, 6, d), lambda i: (row(i), 0, 0)),
            _const_spec((1, d)),
            _layer_spec((d, 3 * d), layer_i),
        ] + [pl.BlockSpec(memory_space=pl.ANY)] * len(prev_caches),
        out_specs=[tok, latent_t, latent_t, cache_blk, cache_blk, prompt_t, prompt_t],
        out_shape=[jax.ShapeDtypeStruct((n_tok, d), BF16), latent_shape, latent_shape, cache_shape, cache_shape,
                   prompt_shape, prompt_shape],
        input_output_aliases={n_in + j: 3 + j for j in range(len(prev_caches))},
        scratch_shapes=[pltpu.VMEM((2 * d, d), BF16)],
        compiler_params=pltpu.CompilerParams(
            dimension_semantics=("arbitrary",), vmem_limit_bytes=VMEM_LIMIT),
        name="qkv",
    )(*x_parts, mod_l, g, w_qkv, *prev_caches)


def _nt_dot(a, b):
    return lax.dot_general(a, b, (((1,), (1,)), ((), ())), preferred_element_type=F32)


def _pair_heads(qp, first):
    zero = jnp.zeros_like(qp)
    return jnp.where(first, qp, zero), jnp.where(first, zero, qp)


def _prompt_attn_kernel(q_ref, kt_ref, vt_ref, o_ref):
    n_seq, d, seq = kt_ref.shape
    half = LANE // 2
    first = lax.broadcasted_iota(jnp.int32, (seq, LANE), 1) < half
    ones = jnp.ones((2 * SUBLANE, seq), BF16)

    def scores(s, p):
        sl = slice(p * LANE, (p + 1) * LANE)
        ktp = kt_ref[s, sl, :]
        return [jnp.dot(qm, ktp, preferred_element_type=F32)
                for qm in _pair_heads(q_ref[s * seq:(s + 1) * seq, sl], first)]

    def weights(pair_scores):
        return [jnp.exp(sc - jnp.max(sc, axis=-1, keepdims=True)).astype(BF16) for sc in pair_scores]

    def finish(s, p, pair_weights):
        sl = slice(p * LANE, (p + 1) * LANE)
        vtp = jnp.concatenate([vt_ref[s, sl, :], ones], axis=0)
        outs = []
        for e in pair_weights:
            ot = _nt_dot(vtp, e)
            outs.append(ot[:LANE, :] / ot[LANE:LANE + 1, :])
        ot = jnp.concatenate([outs[0][:half, :], outs[1][half:, :]], axis=0)
        o_ref[s * seq:(s + 1) * seq, sl] = ot.T.astype(BF16)

    chains = [(s, p) for s in range(n_seq) for p in range(d // LANE)]
    s_next, e_prev = scores(*chains[0]), None
    for n in range(len(chains) + 1):
        s_cur, s_next = s_next, (scores(*chains[n + 1]) if n + 1 < len(chains) else None)
        e_cur = weights(s_cur) if n < len(chains) else None
        if e_prev is not None:
            finish(*chains[n - 1], e_prev)
        e_prev = e_cur


def _prompt_attn(q, kt, vt, *, n_prompt, seq):
    d = q.shape[1]
    n_seq = 4
    tok = pl.BlockSpec((n_seq * seq, d), lambda b: (b, 0))
    feat = pl.BlockSpec((n_seq, d, seq), lambda b: (b, 0, 0))
    return pl.pallas_call(
        _prompt_attn_kernel,
        grid=(n_prompt // (n_seq * seq),),
        in_specs=[tok, feat, feat],
        out_specs=tok,
        out_shape=jax.ShapeDtypeStruct((n_prompt, d), BF16),
        compiler_params=pltpu.CompilerParams(
            dimension_semantics=("arbitrary",), vmem_limit_bytes=VMEM_LIMIT),
        name="prompt_attn",
    )(q, kt, vt)


def _row_start(r, rows):
    kr = min(WIN_ROWS, rows)
    return min(max(r - kr // 2, 0), rows - kr)


def _latent_block_plan(rows):
    kr = min(WIN_ROWS, rows)
    plan = []
    for qb in range(rows // ROWS_PER_BLOCK):
        starts = [_row_start(r, rows) for r in range(qb * ROWS_PER_BLOCK, (qb + 1) * ROWS_PER_BLOCK)]
        plan.append((min(starts) // ROWS_PER_BLOCK, (max(starts) + kr - 1) // ROWS_PER_BLOCK + 1))
    return plan


def _build_pair_bias(rpb_ref, tile_ref, bias_ref, head0, *, plan, rows):
    kr = min(WIN_ROWS, rows)
    shape = (GRID_W, LANE)
    qc = lax.broadcasted_iota(jnp.int32, shape, 0)
    lane = lax.broadcasted_iota(jnp.int32, shape, 1)
    kc = lane & (GRID_W - 1)
    delta = kc - qc + (WIN_COLS - 1)
    col_start = jnp.clip(qc - WIN_COLS // 2, 0, GRID_W - WIN_COLS)
    in_window = (kc >= col_start) & (kc < col_start + WIN_COLS)
    left = lane < GRID_W
    neg = jnp.full(shape, -jnp.inf, F32)
    for hh in range(2):
        base = (head0 + hh) * (N_DROW * N_DCOL)
        for dr in range(N_DROW):
            w = neg
            for dc in range(N_DCOL):
                w = jnp.where(delta == dc, rpb_ref[base + dr * N_DCOL + dc], w)
            tile_ref[dr] = jnp.where(in_window, w, neg)
        t = 0
        for qb, (lo, hi) in enumerate(plan):
            for kb in range(lo, hi):
                for a in range(ROWS_PER_BLOCK):
                    r = qb * ROWS_PER_BLOCK + a
                    start = _row_start(r, rows)
                    for c in range(ATT_BLOCK // LANE):
                        halves = []
                        for jj in range(LANE // GRID_W):
                            k_row = kb * ROWS_PER_BLOCK + c * (LANE // GRID_W) + jj
                            ok = start <= k_row < start + kr
                            halves.append(tile_ref[k_row - r + WIN_ROWS - 1] if ok else None)
                        if halves[0] is None and halves[1] is None:
                            val = neg
                        else:
                            val = jnp.where(left, neg if halves[0] is None else halves[0],
                                            neg if halves[1] is None else halves[1])
                        bias_ref[hh, t, a * GRID_W:(a + 1) * GRID_W, c * LANE:(c + 1) * LANE] = val
                t += 1


def _latent_attn_kernel(rpb_ref, q_ref, kt_ref, vt_ref, kct_ref, vct_ref, wup_ref, wdn_ref, o_ref, wup_o_ref,
                        wdn_o_ref, tile_ref, bias_ref, *, plan, rows, n_batch, n_chunks):
    _cast_chunks(pl.program_id(0) * n_batch + pl.program_id(1), n_chunks, (wup_ref, wdn_ref),
                 (wup_o_ref, wdn_o_ref))

    @pl.when(pl.program_id(1) == 0)
    def _():
        _build_pair_bias(rpb_ref, tile_ref, bias_ref, 2 * pl.program_id(0), plan=plan, rows=rows)

    half = LANE // 2
    first = lax.broadcasted_iota(jnp.int32, (ATT_BLOCK, LANE), 1) < half
    kct = kct_ref[...].astype(BF16)
    vct = jnp.concatenate([vct_ref[...].astype(BF16), jnp.ones((2 * SUBLANE, kct.shape[1]), BF16)], axis=0)
    vt = jnp.concatenate([vt_ref[...], jnp.ones((2 * SUBLANE, vt_ref.shape[1]), BF16)], axis=0)
    bias_starts = [sum(hi - lo for lo, hi in plan[:qb]) for qb in range(len(plan))]

    def scores(qb):
        lo, hi = plan[qb]
        keys = slice(lo * ATT_BLOCK, hi * ATT_BLOCK)
        out = []
        for hh, qm in enumerate(_pair_heads(q_ref[qb * ATT_BLOCK:(qb + 1) * ATT_BLOCK, :], first)):
            bias = jnp.concatenate([bias_ref[hh, bias_starts[qb] + j] for j in range(hi - lo)], axis=1)
            out.append((jnp.dot(qm, kt_ref[:, keys], preferred_element_type=F32) + bias,
                        jnp.dot(qm, kct, preferred_element_type=F32)))
        return out

    def weights(block_scores):
        out = []
        for s_loc, s_ctx in block_scores:
            m = jnp.maximum(jnp.max(s_loc, axis=-1, keepdims=True), jnp.max(s_ctx, axis=-1, keepdims=True))
            out.append((jnp.exp(s_loc - m).astype(BF16), jnp.exp(s_ctx - m).astype(BF16)))
        return out

    def finish(qb, block_weights):
        lo, hi = plan[qb]
        keys = slice(lo * ATT_BLOCK, hi * ATT_BLOCK)
        outs = []
        for e_loc, e_ctx in block_weights:
            ot = _nt_dot(vt[:, keys], e_loc) + _nt_dot(vct, e_ctx)
            outs.append(ot[:LANE, :] / ot[LANE:LANE + 1, :])
        ot = jnp.concatenate([outs[0][:half, :], outs[1][half:, :]], axis=0)
        o_ref[qb * ATT_BLOCK:(qb + 1) * ATT_BLOCK, :] = ot.T.astype(BF16)

    pending = scores(0)
    for qb in range(len(plan)):
        upcoming = scores(qb + 1) if qb + 1 < len(plan) else None
        finish(qb, weights(pending))
        pending = upcoming


def _latent_attn(rpb_flat, q, kt, vt, cache_kt, cache_vt, mlp_weights, *, layer_i, layer, n_prompt, latent_seq):
    d = q.shape[1]
    n_batch, _, _, past = cache_kt.shape
    rows = latent_seq // GRID_W
    plan = _latent_block_plan(rows)
    n_bias = sum(hi - lo for lo, hi in plan)
    first_latent = n_prompt // latent_seq
    n_steps = (d // LANE) * n_batch
    tok_q = pl.BlockSpec((latent_seq, LANE), lambda p, b: (first_latent + b, p))
    feat = pl.BlockSpec((None, LANE, latent_seq), lambda p, b: (b, p, 0))
    ctx = pl.BlockSpec((None, None, LANE, past), lambda p, b: (b, layer_i, p, 0))
    cast_in, cast_out, cast_shapes = _cast_specs(mlp_weights, layer, n_steps, lambda p, b: p * n_batch + b)
    return pl.pallas_call(
        functools.partial(_latent_attn_kernel, plan=plan, rows=rows, n_batch=n_batch, n_chunks=n_steps),
        grid=(d // LANE, n_batch),
        in_specs=[pl.BlockSpec(memory_space=pltpu.SMEM), tok_q, feat, feat, ctx, ctx] + cast_in,
        out_specs=[pl.BlockSpec((latent_seq, LANE), lambda p, b: (b, p))] + cast_out,
        out_shape=[jax.ShapeDtypeStruct((n_batch * latent_seq, d), BF16)] + cast_shapes,
        scratch_shapes=[pltpu.VMEM((N_DROW, GRID_W, LANE), F32),
                        pltpu.VMEM((2, n_bias, ATT_BLOCK, ATT_BLOCK), F32)],
        compiler_params=pltpu.CompilerParams(
            dimension_semantics=("arbitrary", "arbitrary"), vmem_limit_bytes=VMEM_LIMIT),
        name="latent_attn",
    )(rpb_flat, q, kt, vt, cache_kt, cache_vt, *mlp_weights)


def _glu_kernel(x_ref, mod_ref, g_ref, w1_ref, u_ref):
    d = x_ref.shape[1]
    h = _rms_mod(x_ref[...], g_ref[...], mod_ref[0:1, :], mod_ref[1:2, :])
    ag = jnp.dot(h.astype(BF16), w1_ref[...], preferred_element_type=F32)
    u_ref[...] = ag[:, :d] * jax.nn.sigmoid(ag[:, d:])


def _glu_front(x, mod_l, g, w_pw1, *, layer_i, n_prompt, latent_seq, tm):
    n_tok, d = x.shape
    npt = n_prompt // tm
    row = functools.partial(_cond_row, n_prompt_tiles=npt, tiles_per_latent_seq=latent_seq // tm)
    tok = pl.BlockSpec((tm, d), lambda i: (i, 0))
    return pl.pallas_call(
        _glu_kernel,
        grid=(n_tok // tm,),
        in_specs=[tok, pl.BlockSpec((None, 6, d), lambda i: (row(i), 0, 0)), _const_spec((1, d)),
                  _layer_spec((d, 2 * d), layer_i)],
        out_specs=tok,
        out_shape=jax.ShapeDtypeStruct((n_tok, d), F32),
        compiler_params=pltpu.CompilerParams(
            dimension_semantics=("arbitrary",), vmem_limit_bytes=VMEM_LIMIT),
        name="glu_front",
    )(x, mod_l, g, w_pw1)


def _conv_kernel(u_ref, wdw_ref, bdw_ref, lng_ref, lnb_ref, wup_ref, wdn_ref, a_ref,
                 wup_o_ref, wdn_o_ref, pad_ref, *, n_prompt_tiles, prompt_seq, rows_per_step, n_chunks):
    _cast_chunks(pl.program_id(0), n_chunks, (wup_ref, wdn_ref), (wup_o_ref, wdn_o_ref))
    tm, d = u_ref.shape
    u = u_ref[...]
    first_tap = CONV_PAD - CONV_WIDTH // 2
    rs = rows_per_step

    def conv(seq):
        stride = seq + 2 * CONV_PAD
        for s in range(tm // seq):
            base = s * stride
            pad_ref[base:base + CONV_PAD, :] = jnp.zeros((CONV_PAD, d), F32)
            pad_ref[base + CONV_PAD:base + CONV_PAD + seq, :] = u[s * seq:(s + 1) * seq, :]
            pad_ref[base + CONV_PAD + seq:base + stride, :] = jnp.zeros((CONV_PAD, d), F32)
        for s in range(tm // seq):
            base = s * stride

            def step(ci, carry, base=base, s=s):
                r0 = pl.multiple_of(ci * rs, rs)
                parts = []
                for lc in range(d // LANE):
                    ls = slice(lc * LANE, (lc + 1) * LANE)
                    win = pad_ref[pl.ds(base + r0, rs + 2 * CONV_PAD), ls]
                    acc = jnp.broadcast_to(bdw_ref[:, ls], (rs, LANE))
                    for b in range(SUBLANE):
                        vb = None
                        for o in range(b, CONV_WIDTH + first_tap, SUBLANE):
                            t = o - first_tap
                            if t < 0:
                                continue
                            term = win[o - b:o - b + rs + SUBLANE, :] * wdw_ref[t:t + 1, ls]
                            vb = term if vb is None else vb + term
                        if b:
                            vb = pltpu.roll(vb, rs + SUBLANE - b, axis=0)
                        acc = acc + vb[:rs, :]
                    parts.append(acc)
                acc = jnp.concatenate(parts, axis=1)
                mu = jnp.mean(acc, axis=-1, keepdims=True)
                xc = acc - mu
                y = xc * lax.rsqrt(jnp.mean(xc * xc, axis=-1, keepdims=True) + LN_EPS)
                y = y * lng_ref[...] + lnb_ref[...]
                a_ref[pl.ds(s * seq + r0, rs), :] = (y * jax.nn.sigmoid(y)).astype(BF16)
                return carry

            lax.fori_loop(0, seq // rs, step, 0)

    is_prompt = pl.program_id(0) < n_prompt_tiles
    pl.when(is_prompt)(lambda: conv(prompt_seq))
    pl.when(jnp.logical_not(is_prompt))(lambda: conv(tm))


def _conv_front(u, w_dw, b_dw, ln_g, ln_b, mlp_weights, *, layer, n_prompt, prompt_seq, latent_seq):
    n_tok, d = u.shape
    tm = latent_seq
    npt = n_prompt // tm
    n_chunks = 8
    assert n_tok // tm >= n_chunks
    row = functools.partial(_cond_row, n_prompt_tiles=npt, tiles_per_latent_seq=1)
    tok = pl.BlockSpec((tm, d), lambda i: (i, 0))
    pad_rows = (tm // prompt_seq) * (prompt_seq + 2 * CONV_PAD)
    cast_in, cast_out, cast_shapes = _cast_specs(mlp_weights, layer, n_chunks, lambda i: i)
    return pl.pallas_call(
        functools.partial(_conv_kernel, n_prompt_tiles=npt, prompt_seq=prompt_seq, rows_per_step=256,
                          n_chunks=n_chunks),
        grid=(n_tok // tm,),
        in_specs=[
            tok,
            _const_spec((CONV_WIDTH, d)),
            _const_spec((1, d)),
            _const_spec((1, d)),
            _const_spec((1, d)),
        ] + cast_in,
        out_specs=[tok] + cast_out,
        out_shape=[jax.ShapeDtypeStruct((n_tok, d), BF16)] + cast_shapes,
        scratch_shapes=[pltpu.VMEM((pad_rows, d), F32)],
        compiler_params=pltpu.CompilerParams(
            dimension_semantics=("arbitrary",), vmem_limit_bytes=VMEM_LIMIT),
        name="conv_front",
    )(u, w_dw, b_dw, ln_g, ln_b, *mlp_weights)


def _post_kernel(*refs, n_x, n_a, n_out, n_prompt_tiles, ff_chunk, final):
    x_refs, a_refs = refs[:n_x], refs[n_x:n_x + n_a]
    mod_ref, g_ref, wp_ref, wup_ref, wdn_ref, fg_ref = refs[n_x + n_a:n_x + n_a + 6]
    o_refs = refs[n_x + n_a + 6:]
    is_prompt = pl.program_id(0) < n_prompt_tiles
    y = jnp.dot(_token_tile(a_refs, is_prompt), wp_ref[...], preferred_element_type=F32)
    x1 = _token_tile(x_refs, is_prompt) + mod_ref[2:3, :] * y
    h = _rms_mod(x1, g_ref[...], mod_ref[3:4, :], mod_ref[4:5, :]).astype(BF16)
    acc = jnp.zeros_like(x1)
    for c in range(wup_ref.shape[1] // ff_chunk):
        sl = slice(c * ff_chunk, (c + 1) * ff_chunk)
        u = jnp.maximum(jnp.dot(h, wup_ref[:, sl], preferred_element_type=F32), 0.0)
        acc = acc + jnp.dot((u * u).astype(BF16), wdn_ref[sl, :], preferred_element_type=F32)
    out = x1 + mod_ref[5:6, :] * acc
    if final:
        out = out * lax.rsqrt(jnp.mean(out * out, axis=-1, keepdims=True) + RMS_EPS) * fg_ref[...]
    if n_out == 1:
        o_refs[0][...] = out
    else:
        @pl.when(is_prompt)
        def _():
            o_refs[0][...] = out

        @pl.when(jnp.logical_not(is_prompt))
        def _():
            o_refs[1][...] = out


def _post(x_parts, a_parts, mod_l, g, w_proj, w_up, w_down, final_g, *, proj_i, n_prompt, n_tok, latent_seq, tm,
          final):
    d = x_parts[0].shape[1]
    d_ff = w_up.shape[1]
    npt = n_prompt // tm
    row = functools.partial(_cond_row, n_prompt_tiles=npt, tiles_per_latent_seq=latent_seq // tm)
    if final:
        out_shape = [jax.ShapeDtypeStruct((n_prompt, d), F32), jax.ShapeDtypeStruct((n_tok - n_prompt, d), F32)]
    else:
        out_shape = [jax.ShapeDtypeStruct((n_tok, d), F32)]
    return pl.pallas_call(
        functools.partial(_post_kernel, n_x=len(x_parts), n_a=len(a_parts), n_out=len(out_shape),
                          n_prompt_tiles=npt, ff_chunk=1024, final=final),
        grid=(n_tok // tm,),
        in_specs=_token_specs(x_parts, tm, npt) + _token_specs(a_parts, tm, npt) + [
            pl.BlockSpec((None, 6, d), lambda i: (row(i), 0, 0)),
            _const_spec((1, d)),
            _layer_spec((d, d), proj_i),
            _const_spec((d, d_ff)),
            _const_spec((d_ff, d)),
            _const_spec((1, d)),
        ],
        out_specs=_token_specs(out_shape, tm, npt),
        out_shape=out_shape,
        compiler_params=pltpu.CompilerParams(
            dimension_semantics=("arbitrary",), vmem_limit_bytes=VMEM_LIMIT),
        name="post_mlp",
    )(*x_parts, *a_parts, mod_l, g, w_proj, w_up, w_down, final_g)


def kernel(x_prompt, x_sample, cache_k, cache_v, c, c_ctx, norm_g, w_ada, b_ada, w_qkv, w_o, rpb,
           w_pw1, w_dw, b_dw, conv_ln_g, conv_ln_b, w_pw2, w_up, w_down, final_g):
    batch, seq, d = x_prompt.shape
    dec_batch, dec_seq, _ = x_sample.shape
    depth = w_ada.shape[0]
    n_attn = w_qkv.shape[0]
    past = cache_k.shape[2]
    n_prompt = batch * seq
    n_tok = n_prompt + dec_batch * dec_seq
    head_dim = d // N_HEADS
    assert d == N_HEADS * head_dim and 2 * head_dim == LANE and LANE == 2 * GRID_W
    assert seq == ATT_BLOCK and past == ATT_BLOCK and dec_seq % ATT_BLOCK == 0
    assert rpb.shape[1:] == (N_HEADS, N_DROW, N_DCOL)
    assert 1 + dec_batch <= N_COND_ROWS
    tm = 512
    dims = dict(n_prompt=n_prompt, n_tok=n_tok, latent_seq=dec_seq)

    x_parts = [x_prompt.reshape(n_prompt, d), x_sample.reshape(dec_batch * dec_seq, d)]
    cond = jnp.concatenate([c_ctx[None, :], c, jnp.zeros((N_COND_ROWS - 1 - dec_batch, d), F32)], axis=0)
    mod = _adaln(cond, w_ada, b_ada).reshape(depth, N_COND_ROWS, 6, d)
    ckt = jnp.transpose(cache_k, (0, 1, 3, 4, 2)).reshape(dec_batch, n_attn, d, past)
    cvt = jnp.transpose(cache_v, (0, 1, 3, 4, 2)).reshape(dec_batch, n_attn, d, past)
    final_g2 = final_g.reshape(1, d)
    w_qkv_b, w_o_b, w_pw1_b, w_pw2_b = [w.astype(BF16) for w in (w_qkv, w_o, w_pw1, w_pw2)]
    mlp_weights = (w_up, w_down)

    new_caches = ()
    for l in range(depth):
        i = l // 2
        mod_l = mod[l]
        g1 = norm_g[l, 0].reshape(1, d)
        g2 = norm_g[l, 1].reshape(1, d)
        if l % 2 == 0:
            q, kt, vt, ck_new, cv_new, ktp, vtp = _qkv(x_parts, mod_l, g1, w_qkv_b, new_caches, layer_i=i, seq=seq,
                                                       tm=tm, scale=head_dim ** -0.5, **dims)
            new_caches = (ck_new, cv_new)
            o_p = _prompt_attn(q, ktp, vtp, n_prompt=n_prompt, seq=seq)
            o_s, w_up_l, w_down_l = _latent_attn(rpb[i].reshape(-1), q, kt, vt, ckt, cvt, mlp_weights, layer_i=i,
                                                 layer=l, n_prompt=n_prompt, latent_seq=dec_seq)
            a_parts = [o_p, o_s]
            w_proj = w_o_b
        else:
            assert len(x_parts) == 1
            u = _glu_front(x_parts[0], mod_l, g1, w_pw1_b, layer_i=i, n_prompt=n_prompt, latent_seq=dec_seq, tm=tm)
            a, w_up_l, w_down_l = _conv_front(u, w_dw[i], b_dw[i].reshape(1, d), conv_ln_g[i].reshape(1, d),
                                              conv_ln_b[i].reshape(1, d), mlp_weights, layer=l,
                                              n_prompt=n_prompt, prompt_seq=seq, latent_seq=dec_seq)
            a_parts = [a]
            w_proj = w_pw2_b
        x_parts = _post(x_parts, a_parts, mod_l, g2, w_proj, w_up_l, w_down_l, final_g2, proj_i=i, tm=tm,
                        final=(l == depth - 1), **dims)

    y_prompt, y_sample = x_parts
    new_k, new_v = [jnp.transpose(t.reshape(batch, n_attn, N_HEADS, head_dim, seq), (0, 1, 4, 2, 3))
                    for t in new_caches]
    return (y_prompt.reshape(batch, seq, d), y_sample.reshape(dec_batch, dec_seq, d), new_k, new_v)
```
